```python
import math
import jax, jax.numpy as jnp
from jax import lax
import numpy as np

D_MODEL = 2048
BATCH = 2
SEQ = 16384
DEPTH = 2

N_HEADS = 32
N_KV_GROUPS = 4
HEADS_PER_GROUP = N_HEADS // N_KV_GROUPS
D_K = 96
D_V = 64
CMP_BLOCK = 32
CMP_STRIDE = 16
CMP_HIDDEN = 256
SLC_BLOCK = 64
N_SLC = 16
WINDOW = 512
Q_BLOCK = 128
N_BUCKETS = 32
MAX_DISTANCE = 128
POOL_WINDOWS = (2, 4, 8, 16)
N_POOL_GROUPS = len(POOL_WINDOWS)
POOL_GROUP = D_MODEL // N_POOL_GROUPS
D_FF = -(-8 * D_MODEL // (3 * 256)) * 256
EPS = 1e-6
NEG_INF = -1e30
FORCE_SCORE = 1e9
N_NSA_LAYERS = (DEPTH + 1) // 2
N_POOL_LAYERS = DEPTH // 2
Q_WIDTH = N_HEADS * D_K
K_WIDTH = N_KV_GROUPS * D_K
V_WIDTH = N_KV_GROUPS * D_V
GATE_WIDTH = 3 * N_HEADS
NSA_SPLITS = (Q_WIDTH, K_WIDTH, V_WIDTH, K_WIDTH, V_WIDTH, K_WIDTH, V_WIDTH, GATE_WIDTH)
NSA_PROJ = sum(NSA_SPLITS)

kernel_name = 'nsa_pool_interleaved_hybrid'


def rms_norm(x, g):
    x32 = x.astype(jnp.float32)
    y = x32 * lax.rsqrt(jnp.mean(x32 * x32, axis=-1, keepdims=True) + EPS)
    return (y * g.astype(jnp.float32)).astype(x.dtype)


def rel_bucket(dist):
    n = jnp.maximum(dist, 0)
    max_exact = N_BUCKETS // 2
    nf = jnp.maximum(n, 1).astype(jnp.float32)
    large = max_exact + (jnp.log(nf / max_exact) / math.log(MAX_DISTANCE / max_exact)
                         * (N_BUCKETS - max_exact)).astype(jnp.int32)
    large = jnp.minimum(large, N_BUCKETS - 1)
    return jnp.where(n < max_exact, n, large)


def masked_softmax(logits, valid):
    p = jax.nn.softmax(jnp.where(valid, logits, NEG_INF), axis=-1)
    return p * valid


def compress(kv, pos, w1, b1, w2):
    B, S, G, dh = kv.shape
    chunks = kv.reshape(B, S // CMP_STRIDE, CMP_STRIDE, G, dh)
    blocks = jnp.concatenate([chunks[:, :-1], chunks[:, 1:]], axis=2)
    blocks = blocks + pos[None, None, :, None, :]
    nc = blocks.shape[1]
    flat = blocks.transpose(0, 1, 3, 2, 4).reshape(B, nc, G, CMP_BLOCK * dh)
    hid = jax.nn.gelu(flat @ w1 + b1)
    return hid @ w2


def selection_overlap(n_cmp, n_slc):
    i = jnp.arange(n_cmp)[:, None]
    j = jnp.arange(n_slc)[None, :]
    ratio = SLC_BLOCK // CMP_STRIDE
    diff = i - ratio * j
    w = jnp.zeros((n_cmp, n_slc), jnp.float32)
    for n in range(CMP_BLOCK // CMP_STRIDE):
        w = w + ((diff + n >= 0) & (diff + n < ratio)).astype(jnp.float32)
    return w


def nsa_mixer(h, w_in, pos_k, pos_v, ck_w1, ck_b1, ck_w2, cv_w1, cv_b1, cv_w2, w_out, rel_bias):
    B, S, _ = h.shape
    G, HG = N_KV_GROUPS, HEADS_PER_GROUP
    proj = h @ w_in
    offs = []
    acc = 0
    for wdt in NSA_SPLITS[:-1]:
        acc += wdt
        offs.append(acc)
    q, k_c, v_c, k_s, v_s, k_w, v_w, g = jnp.split(proj, offs, axis=-1)
    q = q.reshape(B, S, G, HG, D_K) * (D_K ** -0.5)
    gates = jax.nn.sigmoid(g).reshape(B, S, 3, G, HG)
    k_cmp = compress(k_c.reshape(B, S, G, D_K), pos_k, ck_w1, ck_b1, ck_w2)
    v_cmp = compress(v_c.reshape(B, S, G, D_V), pos_v, cv_w1, cv_b1, cv_w2)
    n_cmp = k_cmp.shape[1]
    n_slc = S // SLC_BLOCK
    n_sel = min(N_SLC, n_slc)
    cmp_end = jnp.arange(n_cmp) * CMP_STRIDE + CMP_BLOCK - 1
    overlap = selection_overlap(n_cmp, n_slc)
    k_sb = k_s.reshape(B, n_slc, SLC_BLOCK, G, D_K).transpose(0, 3, 1, 2, 4)
    v_sb = v_s.reshape(B, n_slc, SLC_BLOCK, G, D_V).transpose(0, 3, 1, 2, 4)
    k_wp = jnp.pad(k_w.reshape(B, S, G, D_K), ((0, 0), (WINDOW, 0), (0, 0), (0, 0)))
    v_wp = jnp.pad(v_w.reshape(B, S, G, D_V), ((0, 0), (WINDOW, 0), (0, 0), (0, 0)))
    tbl = rel_bias.astype(jnp.float32).reshape(N_BUCKETS, G, HG)
    tbl_g = tbl.transpose(1, 0, 2)
    b_ix = jnp.arange(B)[:, None, None, None]
    g_ix = jnp.arange(G)[None, :, None, None]
    j_ix = jnp.arange(n_slc)

    def block(qb):
        q0 = qb * Q_BLOCK
        t = q0 + jnp.arange(Q_BLOCK)
        qblk = lax.dynamic_slice_in_dim(q, q0, Q_BLOCK, axis=1)
        gblk = lax.dynamic_slice_in_dim(gates, q0, Q_BLOCK, axis=1)
        dist_c = t[:, None] - cmp_end[None, :]
        valid_c = dist_c >= 0
        logit_c = (jnp.einsum('bqghd,bngd->bghqn', qblk, k_cmp).astype(jnp.float32)
                   + tbl[rel_bucket(dist_c)].transpose(2, 3, 0, 1))
        p_c = masked_softmax(logit_c, valid_c)
        o_c = jnp.einsum('bghqn,bngd->bqghd', p_c.astype(v_cmp.dtype), v_cmp)
        imp = jnp.einsum('bghqn,nj->bgqj', p_c, overlap)
        cur = (t // SLC_BLOCK)[:, None]
        forced = (j_ix[None, :] == 0) | (j_ix[None, :] == cur) | (j_ix[None, :] == cur - 1)
        imp = jnp.where(forced, FORCE_SCORE, imp)
        imp = jnp.where(j_ix[None, :] > cur, -1.0, imp)
        _, idx = lax.top_k(imp, n_sel)
        k_sel = k_sb[b_ix, g_ix, idx]
        v_sel = v_sb[b_ix, g_ix, idx]
        pos_s = idx[..., None] * SLC_BLOCK + jnp.arange(SLC_BLOCK)
        dist_s = t[None, None, :, None, None] - pos_s
        valid_s = (dist_s >= 0).reshape(B, G, 1, Q_BLOCK, n_sel * SLC_BLOCK)
        bias_s = tbl_g[g_ix[..., None], rel_bucket(dist_s)].transpose(0, 1, 5, 2, 3, 4)
        logit_s = (jnp.einsum('bqghd,bgqktd->bghqkt', qblk, k_sel).astype(jnp.float32) + bias_s)
        logit_s = logit_s.reshape(B, G, HG, Q_BLOCK, n_sel * SLC_BLOCK)
        p_s = masked_softmax(logit_s, valid_s)
        o_s = jnp.einsum('bghqn,bgqnd->bqghd', p_s.astype(v_sel.dtype),
                         v_sel.reshape(B, G, Q_BLOCK, n_sel * SLC_BLOCK, D_V))
        k_blk = lax.dynamic_slice_in_dim(k_wp, q0, Q_BLOCK + WINDOW, axis=1)
        v_blk = lax.dynamic_slice_in_dim(v_wp, q0, Q_BLOCK + WINDOW, axis=1)
        s = q0 - WINDOW + jnp.arange(Q_BLOCK + WINDOW)
        dist_w = t[:, None] - s[None, :]
        valid_w = (s[None, :] >= 0) & (dist_w >= 0) & (dist_w < WINDOW)
        logit_w = (jnp.einsum('bqghd,bkgd->bghqk', qblk, k_blk).astype(jnp.float32)
                   + tbl[rel_bucket(dist_w)].transpose(2, 3, 0, 1))
        p_w = masked_softmax(logit_w, valid_w)
        o_w = jnp.einsum('bghqk,bkgd->bqghd', p_w.astype(v_blk.dtype), v_blk)
        return (gblk[:, :, 0, :, :, None] * o_c + gblk[:, :, 1, :, :, None] * o_s
                + gblk[:, :, 2, :, :, None] * o_w)

    outs = lax.map(block, jnp.arange(S // Q_BLOCK))
    o = outs.transpose(1, 0, 2, 3, 4, 5).reshape(B, S, N_HEADS * D_V)
    return o @ w_out


def pool_mixer(h, w_in, w_grp, scale, w_out):
    B, S, D = h.shape
    u = (h @ w_in).astype(jnp.float32)
    cs = jnp.pad(jnp.cumsum(u, axis=1), ((0, 0), (1, 0), (0, 0)))
    t = jnp.arange(S)
    groups = []
    for gi, w in enumerate(POOL_WINDOWS):
        lo, hi = gi * POOL_GROUP, (gi + 1) * POOL_GROUP
        start = jnp.maximum(t + 1 - w, 0)
        total = cs[:, 1:, lo:hi] - cs[:, start, lo:hi]
        cnt = (t + 1 - start).astype(jnp.float32)
        groups.append(total / cnt[None, :, None] - u[:, :, lo:hi])
    pooled = jnp.stack(groups, axis=2)
    mixed = jnp.einsum('bsgc,gcd->bsgd', pooled, w_grp.astype(jnp.float32)).reshape(B, S, D)
    mixed = (mixed * scale.astype(jnp.float32)).astype(h.dtype)
    return mixed @ w_out


def swiglu(h, w_gu, w_down):
    gu = h @ w_gu
    gate, up = gu[..., :D_FF], gu[..., D_FF:]
    return (jax.nn.silu(gate) * up) @ w_down


def _normal(key, shape, scale):
    return jax.random.normal(key, shape, jnp.float32) * scale


def setup_inputs(seed: int = 0) -> dict:
    key = jax.random.key(seed)
    ks = jax.random.split(key, 24)
    A, P = N_NSA_LAYERS, N_POOL_LAYERS
    return {
        'x': _normal(ks[0], (BATCH, SEQ, D_MODEL), 1.0),
        'norm_mix': 1.0 + _normal(ks[1], (DEPTH, D_MODEL), 0.02),
        'norm_ffn': 1.0 + _normal(ks[2], (DEPTH, D_MODEL), 0.02),
        'norm_final': 1.0 + _normal(ks[3], (D_MODEL,), 0.02),
        'rel_bias': _normal(ks[4], (N_BUCKETS, N_HEADS), 0.2),
        'nsa_w_in': _normal(ks[5], (A, D_MODEL, NSA_PROJ), D_MODEL ** -0.5),
        'nsa_pos_k': _normal(ks[6], (A, CMP_BLOCK, D_K), 0.1),
        'nsa_pos_v': _normal(ks[7], (A, CMP_BLOCK, D_V), 0.1),
        'nsa_ck_w1': _normal(ks[8], (A, CMP_BLOCK * D_K, CMP_HIDDEN), (CMP_BLOCK * D_K) ** -0.5),
        'nsa_ck_b1': _normal(ks[9], (A, CMP_HIDDEN), 0.01),
        'nsa_ck_w2': _normal(ks[10], (A, CMP_HIDDEN, D_K), CMP_HIDDEN ** -0.5),
        'nsa_cv_w1': _normal(ks[11], (A, CMP_BLOCK * D_V, CMP_HIDDEN), (CMP_BLOCK * D_V) ** -0.5),
        'nsa_cv_b1': _normal(ks[12], (A, CMP_HIDDEN), 0.01),
        'nsa_cv_w2': _normal(ks[13], (A, CMP_HIDDEN, D_V), CMP_HIDDEN ** -0.5),
        'nsa_w_out': _normal(ks[14], (A, N_HEADS * D_V, D_MODEL), (N_HEADS * D_V) ** -0.5),
        'pool_w_in': _normal(ks[15], (P, D_MODEL, D_MODEL), D_MODEL ** -0.5),
        'pool_w_grp': _normal(ks[16], (P, N_POOL_GROUPS, POOL_GROUP, POOL_GROUP), POOL_GROUP ** -0.5),
        'pool_scale': 1.0 + _normal(ks[17], (P, D_MODEL), 0.1),
        'pool_w_out': _normal(ks[18], (P, D_MODEL, D_MODEL), D_MODEL ** -0.5),
        'ffn_w_gu': _normal(ks[19], (DEPTH, D_MODEL, 2 * D_FF), D_MODEL ** -0.5),
        'ffn_w_down': _normal(ks[20], (DEPTH, D_FF, D_MODEL), D_FF ** -0.5),
    }


def reference(x, norm_mix, norm_ffn, norm_final, rel_bias, nsa_w_in, nsa_pos_k, nsa_pos_v,
              nsa_ck_w1, nsa_ck_b1, nsa_ck_w2, nsa_cv_w1, nsa_cv_b1, nsa_cv_w2, nsa_w_out,
              pool_w_in, pool_w_grp, pool_scale, pool_w_out, ffn_w_gu, ffn_w_down):
    for i in range(DEPTH):
        h = rms_norm(x, norm_mix[i])
        li = i // 2
        if i % 2 == 0:
            x = x + nsa_mixer(h, nsa_w_in[li], nsa_pos_k[li], nsa_pos_v[li],
                              nsa_ck_w1[li], nsa_ck_b1[li], nsa_ck_w2[li],
                              nsa_cv_w1[li], nsa_cv_b1[li], nsa_cv_w2[li],
                              nsa_w_out[li], rel_bias)
        else:
            x = x + pool_mixer(h, pool_w_in[li], pool_w_grp[li], pool_scale[li], pool_w_out[li])
        h = rms_norm(x, norm_ffn[i])
        x = x + swiglu(h, ffn_w_gu[i], ffn_w_down[i])
    return rms_norm(x, norm_final)
```

```python
import functools
import math

import numpy as np
import jax
import jax.numpy as jnp
from jax import lax
from jax.experimental import pallas as pl
from jax.experimental.pallas import tpu as pltpu

D_MODEL = 2048
N_HEADS = 32
N_KV_GROUPS = 4
HEADS_PER_GROUP = N_HEADS // N_KV_GROUPS
D_K = 96
D_V = 64
CMP_BLOCK = 32
CMP_STRIDE = 16
CMP_HIDDEN = 256
SLC_BLOCK = 64
N_SLC = 16
WINDOW = 512
N_BUCKETS = 32
MAX_DISTANCE = 128
POOL_WINDOWS = (2, 4, 8, 16)
N_POOL_GROUPS = len(POOL_WINDOWS)
POOL_GROUP = D_MODEL // N_POOL_GROUPS
D_FF = -(-8 * D_MODEL // (3 * 256)) * 256
EPS = 1e-6
NEG_INF = -1e30
FORCE_SCORE = 1e9

LANES = 128
HEAD_PAD = LANES
Q_TILE = 256
KEY_CHUNK = 256
CMP_PER_TILE = Q_TILE // CMP_STRIDE
BIAS_K = CMP_PER_TILE * 8
SLOT_Q = 0
SLOT_KS = N_HEADS
SLOT_KW = SLOT_KS + N_KV_GROUPS
SLOT_VS = SLOT_KW + N_KV_GROUPS
SLOT_VW = SLOT_VS + N_KV_GROUPS
N_SLOTS = SLOT_VW + N_KV_GROUPS

F32 = jnp.float32
BF16 = jnp.bfloat16


def _bucket_np(dist):
    n = np.maximum(dist, 0)
    max_exact = N_BUCKETS // 2
    nf = np.maximum(n, 1).astype(np.float32)
    scaled = (np.log(nf / np.float32(max_exact)) / np.float32(math.log(MAX_DISTANCE / max_exact))
              * np.float32(N_BUCKETS - max_exact)).astype(np.float32)
    large = np.minimum(max_exact + scaled.astype(np.int32), N_BUCKETS - 1)
    return np.where(n < max_exact, n, large).astype(np.int32)


_FAR_DIST = int(np.max(np.nonzero(_bucket_np(np.arange(4096)) != N_BUCKETS - 1)[0])) + 1
assert _FAR_DIST <= CMP_STRIDE * 8 + (CMP_STRIDE - 1) - (CMP_BLOCK - 1) + 1 and _FAR_DIST <= KEY_CHUNK
_MASKED_BUCKET = N_BUCKETS
_ZERO_BUCKET = N_BUCKETS + 1


def _toeplitz_bucket_map():
    q = np.arange(Q_TILE)[:, None]
    r = np.arange(2 * KEY_CHUNK)[None, :]
    d = q + KEY_CHUNK - r
    b = _bucket_np(d)
    b = np.where(d >= _FAR_DIST, _ZERO_BUCKET, b)
    return np.where(d < 0, _MASKED_BUCKET, b).astype(np.int32)


def _cmp_bucket_map():
    q = np.arange(Q_TILE)[:, None]
    k = np.arange(BIAS_K)[None, :]
    a, b = q // CMP_STRIDE, q % CMP_STRIDE
    a2, m = k // 8, k % 8 + 1
    d = CMP_STRIDE * m + b - (CMP_BLOCK - 1)
    bk = _bucket_np(d)
    bk = np.where((d >= _FAR_DIST) | (d < 0) | (a != a2), _ZERO_BUCKET, bk)
    return bk.astype(np.int32)


def _params(sem, vmem_mb):
    return pltpu.CompilerParams(dimension_semantics=sem, vmem_limit_bytes=vmem_mb * 1024 * 1024)


def _rms(x, g):
    ms = jnp.mean(x * x, axis=-1, keepdims=True)
    return x * lax.rsqrt(ms + EPS) * g


def _dot(a, b):
    return jnp.dot(a, b, preferred_element_type=F32)


def _dot_nt(a, b):
    return lax.dot_general(a, b, (((1,), (1,)), ((), ())), preferred_element_type=F32)


def _tables_kernel(tbl_ref, bm_ref, bc_ref, tb_ref, lc_ref):
    h = pl.program_id(0) * HEADS_PER_GROUP + pl.program_id(1)
    far = tbl_ref[N_BUCKETS - 1, h]
    bm = bm_ref[...]
    bc = bc_ref[...]
    tb = jnp.where(bm == _MASKED_BUCKET, NEG_INF, 0.0).astype(F32)
    lc = jnp.zeros(bc.shape, F32)
    for b in range(N_BUCKETS - 1):
        val = tbl_ref[b, h] - far
        tb = jnp.where(bm == b, val, tb)
        lc = jnp.where(bc == b, val, lc)
    tb_ref[0] = tb
    lc_ref[0] = lc.astype(BF16)


def _build_tables(rel_bias):
    bm = jnp.asarray(_toeplitz_bucket_map())
    bc = jnp.asarray(_cmp_bucket_map())
    g, hg = N_KV_GROUPS, HEADS_PER_GROUP
    return pl.pallas_call(
        _tables_kernel,
        grid=(g, hg),
        in_specs=[
            pl.BlockSpec(memory_space=pltpu.SMEM),
            pl.BlockSpec(bm.shape, lambda i, j: (0, 0)),
            pl.BlockSpec(bc.shape, lambda i, j: (0, 0)),
        ],
        out_specs=[
            pl.BlockSpec((1, Q_TILE, 2 * KEY_CHUNK), lambda i, j: (i, j, 0)),
            pl.BlockSpec((1, Q_TILE, BIAS_K), lambda i, j: (i, j, 0)),
        ],
        out_shape=[
            jax.ShapeDtypeStruct((g, hg * Q_TILE, 2 * KEY_CHUNK), F32),
            jax.ShapeDtypeStruct((g, hg * Q_TILE, BIAS_K), BF16),
        ],
        compiler_params=_params(("arbitrary", "arbitrary"), 32),
        name="bias_tables",
    )(rel_bias, bm, bc)


def _norm_heads_kernel(x_ref, g_ref, w_ref, sc_ref, ad_ref, o_ref, h_scr, *, slabs):
    @pl.when(pl.program_id(1) == 0)
    def _():
        h_scr[...] = _rms(x_ref[...], g_ref[...]).astype(BF16)

    acc = _dot(h_scr[...], w_ref[...]) * sc_ref[...] + ad_ref[...]
    for k in range(slabs):
        o_ref[k] = acc[:, k * HEAD_PAD:(k + 1) * HEAD_PAD].astype(o_ref.dtype)


def _norm_heads(x, gamma, w, scale_row, add_row, *, tm=512, slabs=4):
    t, d = x.shape
    n = w.shape[1]
    tn = slabs * HEAD_PAD
    return pl.pallas_call(
        functools.partial(_norm_heads_kernel, slabs=slabs),
        grid=(t // tm, n // tn),
        in_specs=[
            pl.BlockSpec((tm, d), lambda i, j: (i, 0)),
            pl.BlockSpec((1, d), lambda i, j: (0, 0)),
            pl.BlockSpec((d, tn), lambda i, j: (0, j)),
            pl.BlockSpec((1, tn), lambda i, j: (0, j)),
            pl.BlockSpec((1, tn), lambda i, j: (0, j)),
        ],
        out_specs=pl.BlockSpec((slabs, tm, HEAD_PAD), lambda i, j: (j, i, 0)),
        out_shape=jax.ShapeDtypeStruct((n // HEAD_PAD, t, HEAD_PAD), BF16),
        scratch_shapes=[pltpu.VMEM((tm, d), BF16)],
        compiler_params=_params(("parallel", "arbitrary"), 48),
        name="norm_proj_heads",
    )(x, gamma, w, scale_row, add_row)


def _norm_side_kernel(x_ref, g_ref, w_ref, kc_ref, vc_ref, gt_ref, *, kw, vw):
    h = _rms(x_ref[...], g_ref[...]).astype(BF16)
    acc = _dot(h, w_ref[...])
    kc_ref[...] = acc[:, :kw]
    vc_ref[...] = acc[:, kw:kw + vw]
    for g in range(N_KV_GROUPS):
        lo = kw + vw + g * LANES
        gt_ref[g] = jax.nn.sigmoid(acc[:, lo:lo + LANES])


def _norm_side(x, gamma, w, *, tm=512):
    t, d = x.shape
    kw, vw = N_KV_GROUPS * D_K, N_KV_GROUPS * D_V
    n = w.shape[1]
    return pl.pallas_call(
        functools.partial(_norm_side_kernel, kw=kw, vw=vw),
        grid=(t // tm,),
        in_specs=[
            pl.BlockSpec((tm, d), lambda i: (i, 0)),
            pl.BlockSpec((1, d), lambda i: (0, 0)),
            pl.BlockSpec((d, n), lambda i: (0, 0)),
        ],
        out_specs=[
            pl.BlockSpec((tm, kw), lambda i: (i, 0)),
            pl.BlockSpec((tm, vw), lambda i: (i, 0)),
            pl.BlockSpec((N_KV_GROUPS, tm, LANES), lambda i: (0, i, 0)),
        ],
        out_shape=[
            jax.ShapeDtypeStruct((t, kw), F32),
            jax.ShapeDtypeStruct((t, vw), F32),
            jax.ShapeDtypeStruct((N_KV_GROUPS, t, LANES), F32),
        ],
        compiler_params=_params(("parallel",), 48),
        name="norm_proj_side",
    )(x, gamma, w)


def _norm_mm_kernel(x_ref, g_ref, w_ref, o_ref, h_scr):
    @pl.when(pl.program_id(1) == 0)
    def _():
        h_scr[...] = _rms(x_ref[...], g_ref[...]).astype(BF16)

    o_ref[...] = _dot(h_scr[...], w_ref[...])


def _norm_mm(x, gamma, w, *, tm=512, tn=1024):
    t, d = x.shape
    n = w.shape[1]
    return pl.pallas_call(
        _norm_mm_kernel,
        grid=(t // tm, n // tn),
        in_specs=[
            pl.BlockSpec((tm, d), lambda i, j: (i, 0)),
            pl.BlockSpec((1, d), lambda i, j: (0, 0)),
            pl.BlockSpec((d, tn), lambda i, j: (0, j)),
        ],
        out_specs=pl.BlockSpec((tm, tn), lambda i, j: (i, j)),
        out_shape=jax.ShapeDtypeStruct((t, n), F32),
        scratch_shapes=[pltpu.VMEM((tm, d), BF16)],
        compiler_params=_params(("parallel", "arbitrary"), 48),
        name="norm_matmul",
    )(x, gamma, w)


def _compress_kernel(x_ref, pa_ref, pb_ref, wa_ref, wb_ref, b1_ref, w2_ref, ad_ref, o_ref,
                     acca, accb, *, nk):
    k = pl.program_id(1)

    @pl.when(k == 0)
    def _():
        acca[...] = jnp.zeros_like(acca)
        accb[...] = jnp.zeros_like(accb)

    x = x_ref[...]
    acca[...] += _dot((x + pa_ref[...]).astype(BF16), wa_ref[...])
    accb[...] += _dot((x + pb_ref[...]).astype(BF16), wb_ref[...])

    @pl.when(k == nk - 1)
    def _():
        rows = accb.shape[0]
        hid = acca[...] + pltpu.roll(accb[...], rows - 1, 0) + b1_ref[...]
        hid = jax.nn.gelu(hid)
        out = _dot(hid.astype(BF16), w2_ref[...]) + ad_ref[...]
        for g in range(N_KV_GROUPS):
            o_ref[g] = out[:, g * HEAD_PAD:(g + 1) * HEAD_PAD].astype(o_ref.dtype)


def _compress(kv, pos, w1, b1, w2, dh, one_col, batch):
    t = kv.shape[0]
    g = N_KV_GROUPS
    chunks = t // CMP_STRIDE
    rows = chunks // batch
    width = CMP_STRIDE * g * dh
    x = kv.reshape(chunks, width)
    eye = jnp.eye(g, dtype=F32)
    w1r = w1.reshape(2, CMP_STRIDE, dh, CMP_HIDDEN)

    def expand(wh):
        return jnp.einsum('rdj,gh->rgdhj', wh, eye).reshape(width, g * CMP_HIDDEN).astype(BF16)

    wa, wb = expand(w1r[0]), expand(w1r[1])
    posr = pos.reshape(2, CMP_STRIDE, 1, dh)
    pa = jnp.broadcast_to(posr[0], (CMP_STRIDE, g, dh)).reshape(1, width)
    pb = jnp.broadcast_to(posr[1], (CMP_STRIDE, g, dh)).reshape(1, width)
    b1r = jnp.tile(b1.reshape(1, CMP_HIDDEN), (1, g))
    w2p = jnp.pad(w2, ((0, 0), (0, HEAD_PAD - dh)))
    w2x = jnp.einsum('jd,gh->gjhd', w2p, eye).reshape(g * CMP_HIDDEN, g * HEAD_PAD).astype(BF16)
    ad = np.zeros((1, g * HEAD_PAD), np.float32)
    if one_col is not None:
        ad[0, one_col::HEAD_PAD] = 1.0
    ad = jnp.asarray(ad)
    kt = width // 8
    nk = width // kt
    return pl.pallas_call(
        functools.partial(_compress_kernel, nk=nk),
        grid=(batch, nk),
        in_specs=[
            pl.BlockSpec((rows, kt), lambda b, k: (b, k)),
            pl.BlockSpec((1, kt), lambda b, k: (0, k)),
            pl.BlockSpec((1, kt), lambda b, k: (0, k)),
            pl.BlockSpec((kt, g * CMP_HIDDEN), lambda b, k: (k, 0)),
            pl.BlockSpec((kt, g * CMP_HIDDEN), lambda b, k: (k, 0)),
            pl.BlockSpec((1, g * CMP_HIDDEN), lambda b, k: (0, 0)),
            pl.BlockSpec((g * CMP_HIDDEN, g * HEAD_PAD), lambda b, k: (0, 0)),
            pl.BlockSpec((1, g * HEAD_PAD), lambda b, k: (0, 0)),
        ],
        out_specs=pl.BlockSpec((g, rows, HEAD_PAD), lambda b, k: (0, b, 0)),
        out_shape=jax.ShapeDtypeStruct((g, chunks, HEAD_PAD), BF16),
        scratch_shapes=[pltpu.VMEM((rows, g * CMP_HIDDEN), F32),
                        pltpu.VMEM((rows, g * CMP_HIDDEN), F32)],
        compiler_params=_params(("parallel", "arbitrary"), 48),
        name="compress_mlp",
    )(x, pa, pb, wa, wb, b1r, w2x, ad)


def _cmp_kernel(q_ref, lc_ref, kc_ref, vc_ref, ov_ref, gt_ref, o_ref, sel_ref, *, n_sel):
    tq = Q_TILE
    t = pl.program_id(2)
    t0 = t * tq
    nc = kc_ref.shape[1]
    ns = ov_ref.shape[1]
    ii = lax.broadcasted_iota(jnp.int32, (nc, BIAS_K), 0)
    kk = lax.broadcasted_iota(jnp.int32, (nc, BIAS_K), 1)
    hot = (ii - CMP_PER_TILE * t) == ((kk >> 3) - ((kk & 7) + 1))
    rhs = jnp.concatenate([kc_ref[0], jnp.where(hot, 1.0, 0.0).astype(BF16)], axis=1)
    qpos = t0 + lax.broadcasted_iota(jnp.int32, (tq, nc), 0)
    cend = lax.broadcasted_iota(jnp.int32, (tq, nc), 1) * CMP_STRIDE + (CMP_BLOCK - 1)
    valid = qpos >= cend
    vc = vc_ref[0]
    gt = gt_ref[0]
    psum = jnp.zeros((tq, nc), F32)
    for hh in range(HEADS_PER_GROUP):
        rows = slice(hh * tq, (hh + 1) * tq)
        lhs = jnp.concatenate([q_ref[hh], lc_ref[0, rows, :]], axis=1)
        s = jnp.where(valid, _dot_nt(lhs, rhs), NEG_INF)
        m = jnp.max(s, axis=-1, keepdims=True)
        e = jnp.where(valid, jnp.exp(s - m), 0.0)
        l = jnp.sum(e, axis=-1, keepdims=True)
        p = e * (1.0 / jnp.where(l > 0.0, l, 1.0))
        psum = psum + p
        o = _dot(p.astype(BF16), vc)
        o_ref[:, hh * D_V:(hh + 1) * D_V] = o[:, :D_V] * gt[:, hh:hh + 1]
    p_hi = psum.astype(BF16)
    p_lo = (psum - p_hi.astype(F32)).astype(BF16)
    ov = ov_ref[...]
    imp = _dot(p_hi, ov) + _dot(p_lo, ov)
    jf = lax.broadcasted_iota(jnp.int32, (tq, ns), 1)
    cur = (t0 + lax.broadcasted_iota(jnp.int32, (tq, ns), 0)) // SLC_BLOCK
    forced = (jf == 0) | (jf == cur) | (jf == cur - 1)
    imp = jnp.where(forced, FORCE_SCORE, imp)
    imp = jnp.where(jf > cur, -1.0, imp)
    jff = jf.astype(F32)

    def pick(_, carry):
        imp, sel = carry
        m = jnp.max(imp, axis=-1, keepdims=True)
        first = jnp.min(jnp.where(imp == m, jff, 1e9), axis=-1, keepdims=True)
        hit = jff == first
        return jnp.where(hit, -jnp.inf, imp), jnp.where(hit, 1.0, sel)

    _, sel = lax.fori_loop(0, n_sel, pick, (imp, jnp.zeros((tq, ns), F32)))
    sel_ref[0] = sel.astype(BF16)


def _cmp_attention(qkv, lc, kcmp, vcmp, ov, gates, batch):
    t = qkv.shape[1]
    s = t // batch
    nc = kcmp.shape[1] // batch
    ns = ov.shape[1]
    nt = s // Q_TILE
    g, hg = N_KV_GROUPS, HEADS_PER_GROUP
    return pl.pallas_call(
        functools.partial(_cmp_kernel, n_sel=min(N_SLC, ns)),
        grid=(batch, g, nt),
        in_specs=[
            pl.BlockSpec((hg, Q_TILE, HEAD_PAD), lambda b, i, j: (i, b * nt + j, 0)),
            pl.BlockSpec((1, hg * Q_TILE, BIAS_K), lambda b, i, j: (i, 0, 0)),
            pl.BlockSpec((1, nc, HEAD_PAD), lambda b, i, j: (i, b, 0)),
            pl.BlockSpec((1, nc, HEAD_PAD), lambda b, i, j: (i, b, 0)),
            pl.BlockSpec((nc, ns), lambda b, i, j: (0, 0)),
            pl.BlockSpec((1, Q_TILE, LANES), lambda b, i, j: (i, b * nt + j, 0)),
        ],
        out_specs=[
            pl.BlockSpec((Q_TILE, hg * D_V), lambda b, i, j: (b * nt + j, i)),
            pl.BlockSpec((1, Q_TILE, ns), lambda b, i, j: (i, b * nt + j, 0)),
        ],
        out_shape=[
            jax.ShapeDtypeStruct((t, N_HEADS * D_V), F32),
            jax.ShapeDtypeStruct((g, t, ns), BF16),
        ],
        compiler_params=_params(("parallel", "parallel", "arbitrary"), 48),
        name="cmp_attention",
    )(qkv, lc, kcmp, vcmp, ov, gates)


def _sel_kernel(q_ref, ks_ref, vs_ref, sel_ref, tb_ref, gt_ref, o_ref, m_scr, acc_scr):
    tq = Q_TILE
    t = pl.program_id(2)
    ns = sel_ref.shape[2]
    q = q_ref[...].reshape(HEADS_PER_GROUP * tq, HEAD_PAD)
    selm = sel_ref[0]
    m_scr[...] = jnp.full(m_scr.shape, NEG_INF, F32)
    acc_scr[...] = jnp.zeros(acc_scr.shape, F32)
    jj = lax.broadcasted_iota(jnp.int32, (ns, KEY_CHUNK), 0)
    kb = lax.broadcasted_iota(jnp.int32, (ns, KEY_CHUNK), 1) // SLC_BLOCK

    def chunk(c, tb_off):
        start = pl.multiple_of(c * KEY_CHUNK, KEY_CHUNK)
        k = ks_ref[0, pl.ds(start, KEY_CHUNK), :]
        v = vs_ref[0, pl.ds(start, KEY_CHUNK), :]
        s = _dot_nt(q, k)
        expand = jnp.where(jj == c * (KEY_CHUNK // SLC_BLOCK) + kb, 1.0, 0.0).astype(BF16)
        picked = _dot(selm, expand) > 0.5
        for hh in range(HEADS_PER_GROUP):
            rows = slice(hh * tq, (hh + 1) * tq)
            sh = s[rows]
            if tb_off is not None:
                sh = sh + tb_ref[0, rows, tb_off:tb_off + KEY_CHUNK]
            sh = jnp.where(picked, sh, NEG_INF)
            m_old = m_scr[rows]
            m_new = jnp.maximum(m_old, jnp.max(sh, axis=-1, keepdims=True))
            alpha = jnp.exp(m_old - m_new)
            p = jnp.exp(sh - m_new)
            acc_scr[rows] = alpha * acc_scr[rows] + _dot(p.astype(BF16), v)
            m_scr[rows] = m_new

    def far_body(c, carry):
        chunk(c, None)
        return carry

    lax.fori_loop(0, jnp.maximum(t - 1, 0), far_body, 0)

    @pl.when(t >= 1)
    def _():
        chunk(t - 1, 0)

    chunk(t, KEY_CHUNK)
    gt = gt_ref[0]
    for hh in range(HEADS_PER_GROUP):
        rows = slice(hh * tq, (hh + 1) * tq)
        acc = acc_scr[rows]
        o = acc[:, :D_V] / acc[:, D_V:D_V + 1]
        o_ref[:, hh * D_V:(hh + 1) * D_V] = o * gt[:, HEADS_PER_GROUP + hh:HEADS_PER_GROUP + hh + 1]


def _sel_attention(qkv, sel, tb, gates, batch):
    t = qkv.shape[1]
    s = t // batch
    ns = sel.shape[2]
    nt = s // Q_TILE
    g, hg = N_KV_GROUPS, HEADS_PER_GROUP
    return pl.pallas_call(
        _sel_kernel,
        grid=(batch, g, nt),
        in_specs=[
            pl.BlockSpec((hg, Q_TILE, HEAD_PAD), lambda b, i, j: (i, b * nt + j, 0)),
            pl.BlockSpec((1, s, HEAD_PAD), lambda b, i, j: (SLOT_KS + i, b, 0)),
            pl.BlockSpec((1, s, HEAD_PAD), lambda b, i, j: (SLOT_VS + i, b, 0)),
            pl.BlockSpec((1, Q_TILE, ns), lambda b, i, j: (i, b * nt + j, 0)),
            pl.BlockSpec((1, hg * Q_TILE, 2 * KEY_CHUNK), lambda b, i, j: (i, 0, 0)),
            pl.BlockSpec((1, Q_TILE, LANES), lambda b, i, j: (i, b * nt + j, 0)),
        ],
        out_specs=pl.BlockSpec((Q_TILE, hg * D_V), lambda b, i, j: (b * nt + j, i)),
        out_shape=jax.ShapeDtypeStruct((t, N_HEADS * D_V), F32),
        scratch_shapes=[pltpu.VMEM((hg * Q_TILE, 1), F32),
                        pltpu.VMEM((hg * Q_TILE, HEAD_PAD), F32)],
        compiler_params=_params(("parallel", "parallel", "arbitrary"), 56),
        name="sel_attention",
    )(qkv, qkv, qkv, sel, tb, gates)


def _win_kernel(q_ref, k0_ref, k1_ref, k2_ref, v0_ref, v1_ref, v2_ref, tb_ref, mw_ref, gt_ref, o_ref):
    tq = Q_TILE
    t = pl.program_id(2)
    pen0 = jnp.where(t >= 2, 0.0, NEG_INF).astype(F32)
    pen1 = jnp.where(t >= 1, 0.0, NEG_INF).astype(F32)
    add0 = mw_ref[...] + pen0
    k0, k1, k2 = k0_ref[0], k1_ref[0], k2_ref[0]
    v0, v1, v2 = v0_ref[0], v1_ref[0], v2_ref[0]
    gt = gt_ref[0]
    for hh in range(HEADS_PER_GROUP):
        rows = slice(hh * tq, (hh + 1) * tq)
        qh = q_ref[hh]
        s0 = _dot_nt(qh, k0) + add0
        s1 = _dot_nt(qh, k1) + (tb_ref[0, rows, 0:KEY_CHUNK] + pen1)
        s2 = _dot_nt(qh, k2) + tb_ref[0, rows, KEY_CHUNK:2 * KEY_CHUNK]
        m = jnp.maximum(jnp.maximum(jnp.max(s0, axis=-1, keepdims=True),
                                    jnp.max(s1, axis=-1, keepdims=True)),
                        jnp.max(s2, axis=-1, keepdims=True))
        acc = (_dot(jnp.exp(s0 - m).astype(BF16), v0) + _dot(jnp.exp(s1 - m).astype(BF16), v1)
               + _dot(jnp.exp(s2 - m).astype(BF16), v2))
        o = acc[:, :D_V] / acc[:, D_V:D_V + 1]
        col = 2 * HEADS_PER_GROUP + hh
        o_ref[:, hh * D_V:(hh + 1) * D_V] = o * gt[:, col:col + 1]


def _win_attention(qkv, tb, gates, batch):
    t = qkv.shape[1]
    s = t // batch
    nt = s // Q_TILE
    g, hg = N_KV_GROUPS, HEADS_PER_GROUP
    assert WINDOW == 2 * KEY_CHUNK and Q_TILE == KEY_CHUNK
    q = np.arange(Q_TILE)[:, None]
    r = np.arange(KEY_CHUNK)[None, :]
    mw = jnp.asarray(np.where(r > q, 0.0, NEG_INF).astype(np.float32))

    def kv_spec(slot, back):
        return pl.BlockSpec((1, KEY_CHUNK, HEAD_PAD),
                            lambda b, i, j: (slot + i, b * nt + jnp.maximum(j - back, 0), 0))

    return pl.pallas_call(
        _win_kernel,
        grid=(batch, g, nt),
        in_specs=[
            pl.BlockSpec((hg, Q_TILE, HEAD_PAD), lambda b, i, j: (i, b * nt + j, 0)),
            kv_spec(SLOT_KW, 2), kv_spec(SLOT_KW, 1), kv_spec(SLOT_KW, 0),
            kv_spec(SLOT_VW, 2), kv_spec(SLOT_VW, 1), kv_spec(SLOT_VW, 0),
            pl.BlockSpec((1, hg * Q_TILE, 2 * KEY_CHUNK), lambda b, i, j: (i, 0, 0)),
            pl.BlockSpec((Q_TILE, KEY_CHUNK), lambda b, i, j: (0, 0)),
            pl.BlockSpec((1, Q_TILE, LANES), lambda b, i, j: (i, b * nt + j, 0)),
        ],
        out_specs=pl.BlockSpec((Q_TILE, hg * D_V), lambda b, i, j: (b * nt + j, i)),
        out_shape=jax.ShapeDtypeStruct((t, N_HEADS * D_V), F32),
        compiler_params=_params(("parallel", "parallel", "arbitrary"), 48),
        name="win_attention",
    )(qkv, qkv, qkv, qkv, qkv, qkv, qkv, tb, mw, gates)


def _out_proj_kernel(a_ref, b_ref, c_ref, w_ref, x_ref, o_ref, s_scr):
    @pl.when(pl.program_id(1) == 0)
    def _():
        s_scr[...] = (a_ref[...] + b_ref[...] + c_ref[...]).astype(BF16)

    o_ref[...] = x_ref[...] + _dot(s_scr[...], w_ref[...])


def _out_proj(a, b, c, w, x, *, tm=256, tn=1024):
    t, d = a.shape
    n = w.shape[1]
    row = pl.BlockSpec((tm, d), lambda i, j: (i, 0))
    return pl.pallas_call(
        _out_proj_kernel,
        grid=(t // tm, n // tn),
        in_specs=[row, row, row,
                  pl.BlockSpec((d, tn), lambda i, j: (0, j)),
                  pl.BlockSpec((tm, tn), lambda i, j: (i, j))],
        out_specs=pl.BlockSpec((tm, tn), lambda i, j: (i, j)),
        out_shape=jax.ShapeDtypeStruct((t, n), F32),
        scratch_shapes=[pltpu.VMEM((tm, d), BF16)],
        compiler_params=_params(("parallel", "arbitrary"), 48),
        name="attn_out_proj",
    )(a, b, c, w, x)


def _ffn_kernel(x_ref, g_ref, wg_ref, wu_ref, wd_ref, gf_ref, o_ref, h_scr, acc_scr, *, nf, final_norm):
    f = pl.program_id(1)

    @pl.when(f == 0)
    def _():
        h_scr[...] = _rms(x_ref[...], g_ref[...]).astype(BF16)
        acc_scr[...] = jnp.zeros_like(acc_scr)

    h = h_scr[...]
    gate = _dot(h, wg_ref[...])
    up = _dot(h, wu_ref[...])
    act = (jax.nn.silu(gate) * up).astype(BF16)
    acc_scr[...] += _dot(act, wd_ref[...])

    @pl.when(f == nf - 1)
    def _():
        y = x_ref[...] + acc_scr[...]
        if final_norm:
            y = _rms(y, gf_ref[...])
        o_ref[...] = y


def _ffn(x, gamma, w_gu, w_down, gamma_final, *, final_norm, tm=512, tf=512):
    t, d = x.shape
    nf = D_FF // tf
    return pl.pallas_call(
        functools.partial(_ffn_kernel, nf=nf, final_norm=final_norm),
        grid=(t // tm, nf),
        in_specs=[
            pl.BlockSpec((tm, d), lambda i, f: (i, 0)),
            pl.BlockSpec((1, d), lambda i, f: (0, 0)),
            pl.BlockSpec((d, tf), lambda i, f: (0, f)),
            pl.BlockSpec((d, tf), lambda i, f: (0, nf + f)),
            pl.BlockSpec((tf, d), lambda i, f: (f, 0)),
            pl.BlockSpec((1, d), lambda i, f: (0, 0)),
        ],
        out_specs=pl.BlockSpec((tm, d), lambda i, f: (i, 0)),
        out_shape=jax.ShapeDtypeStruct((t, d), F32),
        scratch_shapes=[pltpu.VMEM((tm, d), BF16), pltpu.VMEM((tm, d), F32)],
        compiler_params=_params(("parallel", "arbitrary"), 56),
        name="swiglu_ffn",
    )(x, gamma, w_gu, w_gu, w_down, gamma_final)


_POOL_HIST = 16


def _pool_kernel(u_ref, up_ref, x_ref, wg_ref, sc_ref, wo_ref, o_ref, ext_scr, mix_scr, *, tm, seq):
    i = pl.program_id(0)

    @pl.when(pl.program_id(1) == 0)
    def _():
        t0 = (i * tm) % seq
        hist = up_ref[...]
        ext_scr[0:_POOL_HIST, :] = jnp.where(t0 == 0, jnp.zeros_like(hist), hist)
        ext_scr[_POOL_HIST:, :] = u_ref[...]
        tpos = t0 + lax.broadcasted_iota(jnp.int32, (tm, 1), 0)
        for gi, w in enumerate(POOL_WINDOWS):
            cols = slice(gi * POOL_GROUP, (gi + 1) * POOL_GROUP)
            tot = ext_scr[_POOL_HIST:, cols]
            for back in range(1, w):
                tot = tot + ext_scr[_POOL_HIST - back:_POOL_HIST - back + tm, cols]
            cnt = jnp.minimum(tpos + 1, w).astype(F32)
            pooled = tot / cnt - ext_scr[_POOL_HIST:, cols]
            mixed = _dot(pooled.astype(BF16), wg_ref[gi]) * sc_ref[:, cols]
            mix_scr[:, cols] = mixed.astype(BF16)

    o_ref[...] = x_ref[...] + _dot(mix_scr[...], wo_ref[...])


def _pool_mix(u, x, w_grp, scale, w_out, seq, *, tm=512, tn=1024):
    t, d = u.shape
    hb = tm // _POOL_HIST
    return pl.pallas_call(
        functools.partial(_pool_kernel, tm=tm, seq=seq),
        grid=(t // tm, d // tn),
        in_specs=[
            pl.BlockSpec((tm, d), lambda i, j: (i, 0)),
            pl.BlockSpec((_POOL_HIST, d), lambda i, j: (jnp.maximum(i * hb - 1, 0), 0)),
            pl.BlockSpec((tm, tn), lambda i, j: (i, j)),
            pl.BlockSpec((N_POOL_GROUPS, POOL_GROUP, POOL_GROUP), lambda i, j: (0, 0, 0)),
            pl.BlockSpec((1, d), lambda i, j: (0, 0)),
            pl.BlockSpec((d, tn), lambda i, j: (0, j)),
        ],
        out_specs=pl.BlockSpec((tm, tn), lambda i, j: (i, j)),
        out_shape=jax.ShapeDtypeStruct((t, d), F32),
        scratch_shapes=[pltpu.VMEM((tm + _POOL_HIST, d), F32), pltpu.VMEM((tm, d), BF16)],
        compiler_params=_params(("parallel", "arbitrary"), 56),
        name="pool_mixer",
    )(u, u, x, w_grp, scale, w_out)


def _pad_heads(w, n, dh):
    d = w.shape[0]
    return jnp.pad(w.reshape(d, n, dh), ((0, 0), (0, 0), (0, HEAD_PAD - dh))).reshape(d, n * HEAD_PAD)


def _nsa_weights(w_in, rel_bias):
    g, hg = N_KV_GROUPS, HEADS_PER_GROUP
    qw, kw, vw = N_HEADS * D_K, g * D_K, g * D_V
    offs = np.cumsum([0, qw, kw, vw, kw, vw, kw, vw])
    wq, wkc, wvc, wks, wvs, wkw, wvw = [w_in[:, offs[i]:offs[i + 1]] for i in range(7)]
    wgt = w_in[:, offs[7]:]
    w_heads = jnp.concatenate([
        _pad_heads(wq, N_HEADS, D_K), _pad_heads(wks, g, D_K), _pad_heads(wkw, g, D_K),
        _pad_heads(wvs, g, D_V), _pad_heads(wvw, g, D_V)], axis=1).astype(BF16)
    scale = np.ones((N_SLOTS, HEAD_PAD), np.float32)
    scale[:N_HEADS] = D_K ** -0.5
    add = np.zeros((N_SLOTS, HEAD_PAD), np.float32)
    add[SLOT_KS:SLOT_VS, D_K] = 1.0
    add[SLOT_VS:, D_V] = 1.0
    add = jnp.asarray(add).at[:N_HEADS, D_K].set(rel_bias[N_BUCKETS - 1].astype(F32))
    wg = wgt.reshape(-1, 3, g, hg).transpose(0, 2, 1, 3).reshape(-1, g, 3 * hg)
    wg = jnp.pad(wg, ((0, 0), (0, 0), (0, LANES - 3 * hg))).reshape(-1, g * LANES)
    w_side = jnp.concatenate([wkc, wvc, wg], axis=1).astype(BF16)
    return w_heads, jnp.asarray(scale).reshape(1, -1), add.reshape(1, -1), w_side


def _overlap_matrix(nc, ns):
    i = np.arange(nc)[:, None]
    j = np.arange(ns)[None, :]
    ratio = SLC_BLOCK // CMP_STRIDE
    diff = i - ratio * j
    w = np.zeros((nc, ns), np.float32)
    for n in range(CMP_BLOCK // CMP_STRIDE):
        w += ((diff + n >= 0) & (diff + n < ratio)).astype(np.float32)
    w[nc - 1] = 0.0
    return jnp.asarray(w, dtype=BF16)


def _nsa_layer(x, gamma, rel_bias, w_in, pos_k, pos_v, ck_w1, ck_b1, ck_w2, cv_w1, cv_b1, cv_w2,
               w_out, batch):
    t = x.shape[0]
    seq = t // batch
    assert seq % Q_TILE == 0
    w_heads, sc_row, ad_row, w_side = _nsa_weights(w_in, rel_bias)
    tb, lc = _build_tables(rel_bias)
    qkv = _norm_heads(x, gamma, w_heads, sc_row, ad_row)
    kc, vc, gates = _norm_side(x, gamma, w_side)
    kcmp = _compress(kc, pos_k, ck_w1, ck_b1, ck_w2, D_K, D_K, batch)
    vcmp = _compress(vc, pos_v, cv_w1, cv_b1, cv_w2, D_V, None, batch)
    ov = _overlap_matrix(seq // CMP_STRIDE, seq // SLC_BLOCK)
    o_c, sel = _cmp_attention(qkv, lc, kcmp, vcmp, ov, gates, batch)
    o_s = _sel_attention(qkv, sel, tb, gates, batch)
    o_w = _win_attention(qkv, tb, gates, batch)
    return _out_proj(o_c, o_s, o_w, w_out.astype(BF16), x)


def kernel(x, norm_mix, norm_ffn, norm_final, rel_bias, nsa_w_in, nsa_pos_k, nsa_pos_v, nsa_ck_w1, nsa_ck_b1, nsa_ck_w2, nsa_cv_w1, nsa_cv_b1, nsa_cv_w2, nsa_w_out, pool_w_in, pool_w_grp, pool_scale, pool_w_out, ffn_w_gu, ffn_w_down):
    batch, seq, d = x.shape
    h = x.reshape(batch * seq, d)
    gfin = norm_final.reshape(1, d)
    h = _nsa_layer(h, norm_mix[0].reshape(1, d), rel_bias, nsa_w_in[0], nsa_pos_k[0], nsa_pos_v[0],
                   nsa_ck_w1[0], nsa_ck_b1[0], nsa_ck_w2[0], nsa_cv_w1[0], nsa_cv_b1[0], nsa_cv_w2[0],
                   nsa_w_out[0], batch)
    h = _ffn(h, norm_ffn[0].reshape(1, d), ffn_w_gu[0].astype(BF16), ffn_w_down[0].astype(BF16), gfin,
             final_norm=False)
    u = _norm_mm(h, norm_mix[1].reshape(1, d), pool_w_in[0].astype(BF16))
    h = _pool_mix(u, h, pool_w_grp[0].astype(BF16), pool_scale[0].reshape(1, d),
                  pool_w_out[0].astype(BF16), seq)
    h = _ffn(h, norm_ffn[1].reshape(1, d), ffn_w_gu[1].astype(BF16), ffn_w_down[1].astype(BF16), gfin,
             final_norm=True)
    return h.reshape(batch, seq, d)
```

```python
import functools
import math

import numpy as np
import jax
import jax.numpy as jnp
from jax import lax
from jax.experimental import pallas as pl
from jax.experimental.pallas import tpu as pltpu

D_MODEL = 2048
N_HEADS = 32
N_KV_GROUPS = 4
HEADS_PER_GROUP = N_HEADS // N_KV_GROUPS
D_K = 96
D_V = 64
CMP_BLOCK = 32
CMP_STRIDE = 16
CMP_HIDDEN = 256
SLC_BLOCK = 64
N_SLC = 16
WINDOW = 512
N_BUCKETS = 32
MAX_DISTANCE = 128
POOL_WINDOWS = (2, 4, 8, 16)
N_POOL_GROUPS = len(POOL_WINDOWS)
POOL_GROUP = D_MODEL // N_POOL_GROUPS
D_FF = -(-8 * D_MODEL // (3 * 256)) * 256
EPS = 1e-6
NEG_INF = -1e30
FORCE_SCORE = 1e9
LOG2E = math.log2(math.e)

LANES = 128
HEAD_PAD = LANES
Q_TILE = 256
KEY_CHUNK = 256
CMP_PER_TILE = Q_TILE // CMP_STRIDE
BIAS_K = CMP_PER_TILE * 8
SLOT_Q = 0
SLOT_KS = N_HEADS
SLOT_KW = SLOT_KS + N_KV_GROUPS
SLOT_VS = SLOT_KW + N_KV_GROUPS
SLOT_VW = SLOT_VS + N_KV_GROUPS
N_SLOTS = SLOT_VW + N_KV_GROUPS

F32 = jnp.float32
BF16 = jnp.bfloat16


def _bucket_np(dist):
    n = np.maximum(dist, 0)
    max_exact = N_BUCKETS // 2
    nf = np.maximum(n, 1).astype(np.float32)
    scaled = (np.log(nf / np.float32(max_exact)) / np.float32(math.log(MAX_DISTANCE / max_exact))
              * np.float32(N_BUCKETS - max_exact)).astype(np.float32)
    large = np.minimum(max_exact + scaled.astype(np.int32), N_BUCKETS - 1)
    return np.where(n < max_exact, n, large).astype(np.int32)


_FAR_DIST = int(np.max(np.nonzero(_bucket_np(np.arange(4096)) != N_BUCKETS - 1)[0])) + 1
assert _FAR_DIST <= CMP_STRIDE * 8 + (CMP_STRIDE - 1) - (CMP_BLOCK - 1) + 1 and _FAR_DIST <= KEY_CHUNK
_MASKED_BUCKET = N_BUCKETS
_ZERO_BUCKET = N_BUCKETS + 1


def _toeplitz_bucket_map():
    q = np.arange(Q_TILE)[:, None]
    r = np.arange(2 * KEY_CHUNK)[None, :]
    d = q + KEY_CHUNK - r
    b = _bucket_np(d)
    b = np.where(d >= _FAR_DIST, _ZERO_BUCKET, b)
    return np.where(d < 0, _MASKED_BUCKET, b).astype(np.int32)


def _cmp_bucket_map():
    q = np.arange(Q_TILE)[:, None]
    k = np.arange(BIAS_K)[None, :]
    a, b = q // CMP_STRIDE, q % CMP_STRIDE
    a2, m = k // 8, k % 8 + 1
    d = CMP_STRIDE * m + b - (CMP_BLOCK - 1)
    bk = _bucket_np(d)
    bk = np.where((d >= _FAR_DIST) | (d < 0) | (a != a2), _ZERO_BUCKET, bk)
    return bk.astype(np.int32)


def _params(sem, vmem_mb):
    return pltpu.CompilerParams(dimension_semantics=sem, vmem_limit_bytes=vmem_mb * 1024 * 1024)


def _rms(x, g):
    ms = jnp.mean(x * x, axis=-1, keepdims=True)
    return x * lax.rsqrt(ms + EPS) * g


def _dot(a, b):
    return jnp.dot(a, b, preferred_element_type=F32)


def _dot_nt(a, b):
    return lax.dot_general(a, b, (((1,), (1,)), ((), ())), preferred_element_type=F32)


def _tables_kernel(tbl_ref, bm_ref, bc_ref, tb_ref, lc_ref):
    h = pl.program_id(0) * HEADS_PER_GROUP + pl.program_id(1)
    far = tbl_ref[N_BUCKETS - 1, h]
    bm = bm_ref[...]
    bc = bc_ref[...]
    tb = jnp.where(bm == _MASKED_BUCKET, NEG_INF, 0.0).astype(F32)
    lc = jnp.zeros(bc.shape, F32)
    for b in range(N_BUCKETS - 1):
        val = (tbl_ref[b, h] - far) * LOG2E
        tb = jnp.where(bm == b, val, tb)
        lc = jnp.where(bc == b, val, lc)
    tb_ref[0, _TB_NEAR] = tb[:, :KEY_CHUNK]
    tb_ref[0, _TB_DIAG] = tb[:, KEY_CHUNK:]
    lc_ref[0] = lc.astype(BF16)


_TB_NEAR, _TB_DIAG = 0, 1
_TB_SECTIONS = 2


def _build_tables(rel_bias):
    bm = jnp.asarray(_toeplitz_bucket_map())
    bc = jnp.asarray(_cmp_bucket_map())
    g, hg = N_KV_GROUPS, HEADS_PER_GROUP
    return pl.pallas_call(
        _tables_kernel,
        grid=(g, hg),
        in_specs=[
            pl.BlockSpec(memory_space=pltpu.SMEM),
            pl.BlockSpec(bm.shape, lambda i, j: (0, 0)),
            pl.BlockSpec(bc.shape, lambda i, j: (0, 0)),
        ],
        out_specs=[
            pl.BlockSpec((1, _TB_SECTIONS, Q_TILE, KEY_CHUNK), lambda i, j: (i, 0, j, 0)),
            pl.BlockSpec((1, Q_TILE, BIAS_K), lambda i, j: (i, j, 0)),
        ],
        out_shape=[
            jax.ShapeDtypeStruct((g, _TB_SECTIONS, hg * Q_TILE, KEY_CHUNK), F32),
            jax.ShapeDtypeStruct((g, hg * Q_TILE, BIAS_K), BF16),
        ],
        compiler_params=_params(("arbitrary", "arbitrary"), 32),
        name="bias_tables",
    )(rel_bias, bm, bc)


def _norm_heads_kernel(x_ref, g_ref, w_ref, sc_ref, ad_ref, o_ref, h_scr, *, slabs):
    @pl.when(pl.program_id(1) == 0)
    def _():
        h_scr[...] = _rms(x_ref[...], g_ref[...]).astype(BF16)

    acc = _dot(h_scr[...], w_ref[...]) * sc_ref[...] + ad_ref[...]
    for k in range(slabs):
        o_ref[k] = acc[:, k * HEAD_PAD:(k + 1) * HEAD_PAD].astype(o_ref.dtype)


def _norm_heads(x, gamma, w, scale_row, add_row, *, tm=512, slabs=4):
    t, d = x.shape
    n = w.shape[1]
    tn = slabs * HEAD_PAD
    return pl.pallas_call(
        functools.partial(_norm_heads_kernel, slabs=slabs),
        grid=(t // tm, n // tn),
        in_specs=[
            pl.BlockSpec((tm, d), lambda i, j: (i, 0)),
            pl.BlockSpec((1, d), lambda i, j: (0, 0)),
            pl.BlockSpec((d, tn), lambda i, j: (0, j)),
            pl.BlockSpec((1, tn), lambda i, j: (0, j)),
            pl.BlockSpec((1, tn), lambda i, j: (0, j)),
        ],
        out_specs=pl.BlockSpec((slabs, tm, HEAD_PAD), lambda i, j: (j, i, 0)),
        out_shape=jax.ShapeDtypeStruct((n // HEAD_PAD, t, HEAD_PAD), BF16),
        scratch_shapes=[pltpu.VMEM((tm, d), BF16)],
        compiler_params=_params(("parallel", "arbitrary"), 48),
        name="norm_proj_heads",
    )(x, gamma, w, scale_row, add_row)


def _norm_side_kernel(x_ref, g_ref, w_ref, kc_ref, vc_ref, gt_ref, *, kw, vw):
    h = _rms(x_ref[...], g_ref[...]).astype(BF16)
    acc = _dot(h, w_ref[...])
    kc_ref[...] = acc[:, :kw]
    vc_ref[...] = acc[:, kw:kw + vw]
    for g in range(N_KV_GROUPS):
        lo = kw + vw + g * LANES
        gt_ref[g] = jax.nn.sigmoid(acc[:, lo:lo + LANES])


def _norm_side(x, gamma, w, *, tm=512):
    t, d = x.shape
    kw, vw = N_KV_GROUPS * D_K, N_KV_GROUPS * D_V
    n = w.shape[1]
    return pl.pallas_call(
        functools.partial(_norm_side_kernel, kw=kw, vw=vw),
        grid=(t // tm,),
        in_specs=[
            pl.BlockSpec((tm, d), lambda i: (i, 0)),
            pl.BlockSpec((1, d), lambda i: (0, 0)),
            pl.BlockSpec((d, n), lambda i: (0, 0)),
        ],
        out_specs=[
            pl.BlockSpec((tm, kw), lambda i: (i, 0)),
            pl.BlockSpec((tm, vw), lambda i: (i, 0)),
            pl.BlockSpec((N_KV_GROUPS, tm, LANES), lambda i: (0, i, 0)),
        ],
        out_shape=[
            jax.ShapeDtypeStruct((t, kw), F32),
            jax.ShapeDtypeStruct((t, vw), F32),
            jax.ShapeDtypeStruct((N_KV_GROUPS, t, LANES), F32),
        ],
        compiler_params=_params(("parallel",), 48),
        name="norm_proj_side",
    )(x, gamma, w)


def _norm_mm_kernel(x_ref, g_ref, w_ref, o_ref, h_scr):
    @pl.when(pl.program_id(1) == 0)
    def _():
        h_scr[...] = _rms(x_ref[...], g_ref[...]).astype(BF16)

    o_ref[...] = _dot(h_scr[...], w_ref[...])


def _norm_mm(x, gamma, w, *, tm=512, tn=1024):
    t, d = x.shape
    n = w.shape[1]
    return pl.pallas_call(
        _norm_mm_kernel,
        grid=(t // tm, n // tn),
        in_specs=[
            pl.BlockSpec((tm, d), lambda i, j: (i, 0)),
            pl.BlockSpec((1, d), lambda i, j: (0, 0)),
            pl.BlockSpec((d, tn), lambda i, j: (0, j)),
        ],
        out_specs=pl.BlockSpec((tm, tn), lambda i, j: (i, j)),
        out_shape=jax.ShapeDtypeStruct((t, n), F32),
        scratch_shapes=[pltpu.VMEM((tm, d), BF16)],
        compiler_params=_params(("parallel", "arbitrary"), 48),
        name="norm_matmul",
    )(x, gamma, w)


def _compress_kernel(x_ref, pa_ref, pb_ref, wa_ref, wb_ref, b1_ref, w2_ref, ad_ref, o_ref,
                     acca, accb, *, nk):
    k = pl.program_id(1)

    @pl.when(k == 0)
    def _():
        acca[...] = jnp.zeros_like(acca)
        accb[...] = jnp.zeros_like(accb)

    x = x_ref[...]
    acca[...] += _dot((x + pa_ref[...]).astype(BF16), wa_ref[...])
    accb[...] += _dot((x + pb_ref[...]).astype(BF16), wb_ref[...])

    @pl.when(k == nk - 1)
    def _():
        rows = accb.shape[0]
        hid = acca[...] + pltpu.roll(accb[...], rows - 1, 0) + b1_ref[...]
        hid = jax.nn.gelu(hid)
        out = _dot(hid.astype(BF16), w2_ref[...]) + ad_ref[...]
        for g in range(N_KV_GROUPS):
            o_ref[g] = out[:, g * HEAD_PAD:(g + 1) * HEAD_PAD].astype(o_ref.dtype)


def _compress(kv, pos, w1, b1, w2, dh, one_col, batch):
    t = kv.shape[0]
    g = N_KV_GROUPS
    chunks = t // CMP_STRIDE
    rows = chunks // batch
    width = CMP_STRIDE * g * dh
    x = kv.reshape(chunks, width)
    eye = jnp.eye(g, dtype=F32)
    w1r = w1.reshape(2, CMP_STRIDE, dh, CMP_HIDDEN)

    def expand(wh):
        return jnp.einsum('rdj,gh->rgdhj', wh, eye).reshape(width, g * CMP_HIDDEN).astype(BF16)

    wa, wb = expand(w1r[0]), expand(w1r[1])
    posr = pos.reshape(2, CMP_STRIDE, 1, dh)
    pa = jnp.broadcast_to(posr[0], (CMP_STRIDE, g, dh)).reshape(1, width)
    pb = jnp.broadcast_to(posr[1], (CMP_STRIDE, g, dh)).reshape(1, width)
    b1r = jnp.tile(b1.reshape(1, CMP_HIDDEN), (1, g))
    w2p = jnp.pad(w2, ((0, 0), (0, HEAD_PAD - dh)))
    w2x = jnp.einsum('jd,gh->gjhd', w2p, eye).reshape(g * CMP_HIDDEN, g * HEAD_PAD).astype(BF16)
    ad = np.zeros((1, g * HEAD_PAD), np.float32)
    if one_col is not None:
        ad[0, one_col::HEAD_PAD] = 1.0
    ad = jnp.asarray(ad)
    kt = width // 8
    nk = width // kt
    return pl.pallas_call(
        functools.partial(_compress_kernel, nk=nk),
        grid=(batch, nk),
        in_specs=[
            pl.BlockSpec((rows, kt), lambda b, k: (b, k)),
            pl.BlockSpec((1, kt), lambda b, k: (0, k)),
            pl.BlockSpec((1, kt), lambda b, k: (0, k)),
            pl.BlockSpec((kt, g * CMP_HIDDEN), lambda b, k: (k, 0)),
            pl.BlockSpec((kt, g * CMP_HIDDEN), lambda b, k: (k, 0)),
            pl.BlockSpec((1, g * CMP_HIDDEN), lambda b, k: (0, 0)),
            pl.BlockSpec((g * CMP_HIDDEN, g * HEAD_PAD), lambda b, k: (0, 0)),
            pl.BlockSpec((1, g * HEAD_PAD), lambda b, k: (0, 0)),
        ],
        out_specs=pl.BlockSpec((g, rows, HEAD_PAD), lambda b, k: (0, b, 0)),
        out_shape=jax.ShapeDtypeStruct((g, chunks, HEAD_PAD), BF16),
        scratch_shapes=[pltpu.VMEM((rows, g * CMP_HIDDEN), F32),
                        pltpu.VMEM((rows, g * CMP_HIDDEN), F32)],
        compiler_params=_params(("parallel", "arbitrary"), 48),
        name="compress_mlp",
    )(x, pa, pb, wa, wb, b1r, w2x, ad)


def _cmp_kernel(q_ref, lc_ref, kc_ref, vc_ref, ov_ref, gt_ref, o_ref, sel_ref, s_scr, p_scr, *, n_sel):
    tq = Q_TILE
    t = pl.program_id(2)
    t0 = t * tq
    nc = kc_ref.shape[1]
    ns = ov_ref.shape[1]
    ii = lax.broadcasted_iota(jnp.int32, (nc, BIAS_K), 0)
    kk = lax.broadcasted_iota(jnp.int32, (nc, BIAS_K), 1)
    hot = (ii - CMP_PER_TILE * t) == ((kk >> 3) - ((kk & 7) + 1))
    rhs = jnp.concatenate([kc_ref[0], jnp.where(hot, 1.0, 0.0).astype(BF16)], axis=1)
    qpos = t0 + lax.broadcasted_iota(jnp.int32, (tq, nc), 0)
    cend = lax.broadcasted_iota(jnp.int32, (tq, nc), 1) * CMP_STRIDE + (CMP_BLOCK - 1)
    valid = qpos >= cend
    q = q_ref[...].reshape(HEADS_PER_GROUP * tq, HEAD_PAD)
    s_scr[...] = _dot_nt(jnp.concatenate([q, lc_ref[0]], axis=1), rhs)
    psum = jnp.zeros((tq, nc), F32)
    for hh in range(HEADS_PER_GROUP):
        rows = slice(hh * tq, (hh + 1) * tq)
        s = jnp.where(valid, s_scr[rows], NEG_INF)
        m = jnp.max(s, axis=-1, keepdims=True)
        e = jnp.where(valid, jnp.exp2(s - m), 0.0)
        l = jnp.sum(e, axis=-1, keepdims=True)
        p = e * (1.0 / jnp.where(l > 0.0, l, 1.0))
        psum = psum + p
        p_scr[rows] = p.astype(BF16)
    o = _dot(p_scr[...], vc_ref[0])
    gt = gt_ref[0]
    for hh in range(HEADS_PER_GROUP):
        o_ref[:, hh * D_V:(hh + 1) * D_V] = o[hh * tq:(hh + 1) * tq, :D_V] * gt[:, hh:hh + 1]
    p_hi = psum.astype(BF16)
    p_lo = (psum - p_hi.astype(F32)).astype(BF16)
    ov = ov_ref[...]
    imp = _dot(p_hi, ov) + _dot(p_lo, ov)
    jf = lax.broadcasted_iota(jnp.int32, (tq, ns), 1)
    cur = (t0 + lax.broadcasted_iota(jnp.int32, (tq, ns), 0)) // SLC_BLOCK
    forced = (jf == 0) | (jf == cur) | (jf == cur - 1)
    imp = jnp.where(forced, FORCE_SCORE, imp)
    imp = jnp.where(jf > cur, -1.0, imp)
    jff = jf.astype(F32)

    def pick(_, carry):
        imp, sel = carry
        m = jnp.max(imp, axis=-1, keepdims=True)
        first = jnp.min(jnp.where(imp == m, jff, 1e9), axis=-1, keepdims=True)
        hit = jff == first
        return jnp.where(hit, -jnp.inf, imp), jnp.where(hit, 1.0, sel)

    _, sel = lax.fori_loop(0, n_sel, pick, (imp, jnp.zeros((tq, ns), F32)))
    if sel_ref.shape[2] == ns:
        sel_ref[0] = sel.astype(BF16)
    else:
        sel_ref[0] = jnp.zeros(sel_ref.shape[1:], BF16)
        sel_ref[0, :, 0:ns] = sel.astype(BF16)


def _cmp_attention(qkv, lc, kcmp, vcmp, ov, gates, batch):
    t = qkv.shape[1]
    s = t // batch
    nc = kcmp.shape[1] // batch
    ns = ov.shape[1]
    nt = s // Q_TILE
    g, hg = N_KV_GROUPS, HEADS_PER_GROUP
    ns_pad = -(-ns // LANES) * LANES
    return pl.pallas_call(
        functools.partial(_cmp_kernel, n_sel=min(N_SLC, ns)),
        grid=(batch, g, nt),
        in_specs=[
            pl.BlockSpec((hg, Q_TILE, HEAD_PAD), lambda b, i, j: (i, b * nt + j, 0)),
            pl.BlockSpec((1, hg * Q_TILE, BIAS_K), lambda b, i, j: (i, 0, 0)),
            pl.BlockSpec((1, nc, HEAD_PAD), lambda b, i, j: (i, b, 0)),
            pl.BlockSpec((1, nc, HEAD_PAD), lambda b, i, j: (i, b, 0)),
            pl.BlockSpec((nc, ns), lambda b, i, j: (0, 0)),
            pl.BlockSpec((1, Q_TILE, LANES), lambda b, i, j: (i, b * nt + j, 0)),
        ],
        out_specs=[
            pl.BlockSpec((Q_TILE, hg * D_V), lambda b, i, j: (b * nt + j, i)),
            pl.BlockSpec((1, Q_TILE, ns_pad), lambda b, i, j: (i, b * nt + j, 0)),
        ],
        out_shape=[
            jax.ShapeDtypeStruct((t, N_HEADS * D_V), F32),
            jax.ShapeDtypeStruct((g, t, ns_pad), BF16),
        ],
        scratch_shapes=[pltpu.VMEM((hg * Q_TILE, nc), F32), pltpu.VMEM((hg * Q_TILE, nc), BF16)],
        compiler_params=_params(("parallel", "parallel", "arbitrary"), 48),
        name="cmp_attention",
    )(qkv, lc, kcmp, vcmp, ov, gates)


def _sel_kernel(q_ref, ks_ref, vs_ref, sel_ref, tb_ref, gt_ref, o_ref,
                lhs_scr, s_scr, smax_scr, m_scr, acc_scr):
    tq = Q_TILE
    t = pl.program_id(2)
    n_var = sel_ref.shape[2] // LANES
    chunks_per_var = LANES * SLC_BLOCK // KEY_CHUNK
    blocks_per_chunk = KEY_CHUNK // SLC_BLOCK
    unsel = (1.0 - sel_ref[0].astype(F32)).astype(BF16)
    for var in range(n_var):
        @pl.when(t >= var * chunks_per_var)
        def _():
            for hh in range(HEADS_PER_GROUP):
                rows = slice(hh * tq, (hh + 1) * tq)
                lhs_scr[var, rows, 0:HEAD_PAD] = q_ref[hh]
                lhs_scr[var, rows, HEAD_PAD:] = unsel[:, var * LANES:(var + 1) * LANES]
    m_scr[...] = jnp.full(m_scr.shape, NEG_INF, F32)
    acc_scr[...] = jnp.zeros(acc_scr.shape, F32)
    lane = lax.broadcasted_iota(jnp.int32, (KEY_CHUNK, LANES), 1)
    kblk = lax.broadcasted_iota(jnp.int32, (KEY_CHUNK, LANES), 0) // SLC_BLOCK

    n_far = jnp.maximum(t - 1, 0)

    def qk(c, phantom):
        cc = jnp.minimum(c, t)
        start = pl.multiple_of(cc * KEY_CHUNK, KEY_CHUNK)
        k = ks_ref[0, pl.ds(start, KEY_CHUNK), :]
        flag = (cc % chunks_per_var) * blocks_per_chunk + kblk
        armed = lane == flag
        if phantom is not None:
            armed = armed | phantom
        rhs = jnp.concatenate([k, jnp.where(armed, NEG_INF, 0.0).astype(BF16)], axis=1)
        return _dot_nt(lhs_scr[cc // chunks_per_var], rhs)

    def accumulate(c, s, smax):
        cc = jnp.minimum(c, t)
        start = pl.multiple_of(cc * KEY_CHUNK, KEY_CHUNK)
        v = vs_ref[0, pl.ds(start, KEY_CHUNK), :]
        m_old = m_scr[...]
        m_new = jnp.maximum(m_old, smax)
        alpha = jnp.exp2(m_old - m_new)
        p = jnp.exp2(s - jnp.concatenate([m_new] * (KEY_CHUNK // LANES), axis=1))
        acc_scr[...] = alpha * acc_scr[...] + _dot(p.astype(BF16), v)
        m_scr[...] = m_new

    def scores(c, buf):
        s = qk(c, c >= n_far)
        s_scr[buf] = s
        smax_scr[buf] = jnp.broadcast_to(jnp.max(s, axis=-1, keepdims=True), smax_scr.shape[1:])

    def update(c, buf):
        accumulate(c, s_scr[buf], smax_scr[buf])

    def near(c, section):
        s = qk(c, None) + tb_ref[0, section]
        accumulate(c, s, jnp.max(s, axis=-1, keepdims=True))

    scores(0, 0)

    def pair(i, carry):
        c = 2 * i
        scores(c + 1, 1)
        update(c, 0)
        scores(c + 2, 0)
        update(c + 1, 1)
        return carry

    lax.fori_loop(0, (n_far + 1) // 2, pair, 0)

    @pl.when(t >= 1)
    def _():
        near(t - 1, _TB_NEAR)

    near(t, _TB_DIAG)
    gt = gt_ref[0]
    for hh in range(HEADS_PER_GROUP):
        rows = slice(hh * tq, (hh + 1) * tq)
        acc = acc_scr[rows]
        o = acc[:, :D_V] / acc[:, D_V:D_V + 1]
        o_ref[:, hh * D_V:(hh + 1) * D_V] = o * gt[:, HEADS_PER_GROUP + hh:HEADS_PER_GROUP + hh + 1]


def _sel_attention(qkv, sel, tb, gates, batch):
    t = qkv.shape[1]
    s = t // batch
    ns = sel.shape[2]
    nt = s // Q_TILE
    g, hg = N_KV_GROUPS, HEADS_PER_GROUP
    assert ns % LANES == 0 and HEAD_PAD == LANES
    return pl.pallas_call(
        _sel_kernel,
        grid=(batch, g, nt),
        in_specs=[
            pl.BlockSpec((hg, Q_TILE, HEAD_PAD), lambda b, i, j: (i, b * nt + j, 0)),
            pl.BlockSpec((1, s, HEAD_PAD), lambda b, i, j: (SLOT_KS + i, b, 0)),
            pl.BlockSpec((1, s, HEAD_PAD), lambda b, i, j: (SLOT_VS + i, b, 0)),
            pl.BlockSpec((1, Q_TILE, ns), lambda b, i, j: (i, b * nt + j, 0)),
            pl.BlockSpec((1, _TB_SECTIONS, hg * Q_TILE, KEY_CHUNK), lambda b, i, j: (i, 0, 0, 0)),
            pl.BlockSpec((1, Q_TILE, LANES), lambda b, i, j: (i, b * nt + j, 0)),
        ],
        out_specs=pl.BlockSpec((Q_TILE, hg * D_V), lambda b, i, j: (b * nt + j, i)),
        out_shape=jax.ShapeDtypeStruct((t, N_HEADS * D_V), F32),
        scratch_shapes=[pltpu.VMEM((ns // LANES, hg * Q_TILE, HEAD_PAD + LANES), BF16),
                        pltpu.VMEM((2, hg * Q_TILE, KEY_CHUNK), F32),
                        pltpu.VMEM((2, hg * Q_TILE, LANES), F32),
                        pltpu.VMEM((hg * Q_TILE, LANES), F32),
                        pltpu.VMEM((hg * Q_TILE, HEAD_PAD), F32)],
        compiler_params=_params(("parallel", "parallel", "arbitrary"), 60),
        name="sel_attention",
    )(qkv, qkv, qkv, sel, tb, gates)


def _win_kernel(q_ref, k0_ref, k1_ref, k2_ref, v0_ref, v1_ref, v2_ref, tb_ref, mw_ref, gt_ref, o_ref):
    tq = Q_TILE
    t = pl.program_id(2)
    pen0 = jnp.where(t >= 2, 0.0, NEG_INF).astype(F32)
    pen1 = jnp.where(t >= 1, 0.0, NEG_INF).astype(F32)
    q = q_ref[...].reshape(HEADS_PER_GROUP * tq, HEAD_PAD)
    s0 = _dot_nt(q, k0_ref[0]) + (mw_ref[...] + pen0)
    s1 = _dot_nt(q, k1_ref[0]) + (tb_ref[0, _TB_NEAR] + pen1)
    s2 = _dot_nt(q, k2_ref[0]) + tb_ref[0, _TB_DIAG]
    m = jnp.maximum(jnp.maximum(jnp.max(s0, axis=-1, keepdims=True),
                                jnp.max(s1, axis=-1, keepdims=True)),
                    jnp.max(s2, axis=-1, keepdims=True))
    acc = (_dot(jnp.exp2(s0 - m).astype(BF16), v0_ref[0]) + _dot(jnp.exp2(s1 - m).astype(BF16), v1_ref[0])
           + _dot(jnp.exp2(s2 - m).astype(BF16), v2_ref[0]))
    o = acc[:, :D_V] / acc[:, D_V:D_V + 1]
    gt = gt_ref[0]
    for hh in range(HEADS_PER_GROUP):
        col = 2 * HEADS_PER_GROUP + hh
        o_ref[:, hh * D_V:(hh + 1) * D_V] = o[hh * tq:(hh + 1) * tq] * gt[:, col:col + 1]


def _win_attention(qkv, tb, gates, batch):
    t = qkv.shape[1]
    s = t // batch
    nt = s // Q_TILE
    g, hg = N_KV_GROUPS, HEADS_PER_GROUP
    assert WINDOW == 2 * KEY_CHUNK and Q_TILE == KEY_CHUNK
    q = np.arange(Q_TILE)[:, None]
    r = np.arange(KEY_CHUNK)[None, :]
    mw = jnp.asarray(np.tile(np.where(r > q, 0.0, NEG_INF).astype(np.float32), (hg, 1)))

    def kv_spec(slot, back):
        return pl.BlockSpec((1, KEY_CHUNK, HEAD_PAD),
                            lambda b, i, j: (slot + i, b * nt + jnp.maximum(j - back, 0), 0))

    return pl.pallas_call(
        _win_kernel,
        grid=(batch, g, nt),
        in_specs=[
            pl.BlockSpec((hg, Q_TILE, HEAD_PAD), lambda b, i, j: (i, b * nt + j, 0)),
            kv_spec(SLOT_KW, 2), kv_spec(SLOT_KW, 1), kv_spec(SLOT_KW, 0),
            kv_spec(SLOT_VW, 2), kv_spec(SLOT_VW, 1), kv_spec(SLOT_VW, 0),
            pl.BlockSpec((1, 2, hg * Q_TILE, KEY_CHUNK), lambda b, i, j: (i, 0, 0, 0)),
            pl.BlockSpec((hg * Q_TILE, KEY_CHUNK), lambda b, i, j: (0, 0)),
            pl.BlockSpec((1, Q_TILE, LANES), lambda b, i, j: (i, b * nt + j, 0)),
        ],
        out_specs=pl.BlockSpec((Q_TILE, hg * D_V), lambda b, i, j: (b * nt + j, i)),
        out_shape=jax.ShapeDtypeStruct((t, N_HEADS * D_V), F32),
        compiler_params=_params(("parallel", "parallel", "arbitrary"), 48),
        name="win_attention",
    )(qkv, qkv, qkv, qkv, qkv, qkv, qkv, tb, mw, gates)


def _out_proj_kernel(a_ref, b_ref, c_ref, w_ref, x_ref, o_ref, s_scr):
    @pl.when(pl.program_id(1) == 0)
    def _():
        s_scr[...] = (a_ref[...] + b_ref[...] + c_ref[...]).astype(BF16)

    o_ref[...] = x_ref[...] + _dot(s_scr[...], w_ref[...])


def _out_proj(a, b, c, w, x, *, tm=256, tn=1024):
    t, d = a.shape
    n = w.shape[1]
    row = pl.BlockSpec((tm, d), lambda i, j: (i, 0))
    return pl.pallas_call(
        _out_proj_kernel,
        grid=(t // tm, n // tn),
        in_specs=[row, row, row,
                  pl.BlockSpec((d, tn), lambda i, j: (0, j)),
                  pl.BlockSpec((tm, tn), lambda i, j: (i, j))],
        out_specs=pl.BlockSpec((tm, tn), lambda i, j: (i, j)),
        out_shape=jax.ShapeDtypeStruct((t, n), F32),
        scratch_shapes=[pltpu.VMEM((tm, d), BF16)],
        compiler_params=_params(("parallel", "arbitrary"), 48),
        name="attn_out_proj",
    )(a, b, c, w, x)


def _ffn_kernel(x_ref, g_ref, wg_ref, wu_ref, wd_ref, gf_ref, o_ref, h_scr, acc_scr, *, nf, final_norm):
    f = pl.program_id(1)

    @pl.when(f == 0)
    def _():
        h_scr[...] = _rms(x_ref[...], g_ref[...]).astype(BF16)
        acc_scr[...] = jnp.zeros_like(acc_scr)

    h = h_scr[...]
    gate = _dot(h, wg_ref[...])
    up = _dot(h, wu_ref[...])
    act = (jax.nn.silu(gate) * up).astype(BF16)
    acc_scr[...] += _dot(act, wd_ref[...])

    @pl.when(f == nf - 1)
    def _():
        y = x_ref[...] + acc_scr[...]
        if final_norm:
            y = _rms(y, gf_ref[...])
        o_ref[...] = y


def _ffn(x, gamma, w_gu, w_down, gamma_final, *, final_norm, tm=512, tf=512):
    t, d = x.shape
    nf = D_FF // tf
    return pl.pallas_call(
        functools.partial(_ffn_kernel, nf=nf, final_norm=final_norm),
        grid=(t // tm, nf),
        in_specs=[
            pl.BlockSpec((tm, d), lambda i, f: (i, 0)),
            pl.BlockSpec((1, d), lambda i, f: (0, 0)),
            pl.BlockSpec((d, tf), lambda i, f: (0, f)),
            pl.BlockSpec((d, tf), lambda i, f: (0, nf + f)),
            pl.BlockSpec((tf, d), lambda i, f: (f, 0)),
            pl.BlockSpec((1, d), lambda i, f: (0, 0)),
        ],
        out_specs=pl.BlockSpec((tm, d), lambda i, f: (i, 0)),
        out_shape=jax.ShapeDtypeStruct((t, d), F32),
        scratch_shapes=[pltpu.VMEM((tm, d), BF16), pltpu.VMEM((tm, d), F32)],
        compiler_params=_params(("parallel", "arbitrary"), 56),
        name="swiglu_ffn",
    )(x, gamma, w_gu, w_gu, w_down, gamma_final)


_POOL_HIST = 16


def _pool_kernel(u_ref, up_ref, x_ref, wg_ref, sc_ref, wo_ref, o_ref, ext_scr, mix_scr, *, tm, seq):
    i = pl.program_id(0)

    @pl.when(pl.program_id(1) == 0)
    def _():
        t0 = (i * tm) % seq
        hist = up_ref[...]
        ext_scr[0:_POOL_HIST, :] = jnp.where(t0 == 0, jnp.zeros_like(hist), hist)
        ext_scr[_POOL_HIST:, :] = u_ref[...]
        tpos = t0 + lax.broadcasted_iota(jnp.int32, (tm, 1), 0)
        for gi, w in enumerate(POOL_WINDOWS):
            cols = slice(gi * POOL_GROUP, (gi + 1) * POOL_GROUP)
            tot = ext_scr[_POOL_HIST:, cols]
            for back in range(1, w):
                tot = tot + ext_scr[_POOL_HIST - back:_POOL_HIST - back + tm, cols]
            cnt = jnp.minimum(tpos + 1, w).astype(F32)
            pooled = tot / cnt - ext_scr[_POOL_HIST:, cols]
            mixed = _dot(pooled.astype(BF16), wg_ref[gi]) * sc_ref[:, cols]
            mix_scr[:, cols] = mixed.astype(BF16)

    o_ref[...] = x_ref[...] + _dot(mix_scr[...], wo_ref[...])


def _pool_mix(u, x, w_grp, scale, w_out, seq, *, tm=512, tn=1024):
    t, d = u.shape
    hb = tm // _POOL_HIST
    return pl.pallas_call(
        functools.partial(_pool_kernel, tm=tm, seq=seq),
        grid=(t // tm, d // tn),
        in_specs=[
            pl.BlockSpec((tm, d), lambda i, j: (i, 0)),
            pl.BlockSpec((_POOL_HIST, d), lambda i, j: (jnp.maximum(i * hb - 1, 0), 0)),
            pl.BlockSpec((tm, tn), lambda i, j: (i, j)),
            pl.BlockSpec((N_POOL_GROUPS, POOL_GROUP, POOL_GROUP), lambda i, j: (0, 0, 0)),
            pl.BlockSpec((1, d), lambda i, j: (0, 0)),
            pl.BlockSpec((d, tn), lambda i, j: (0, j)),
        ],
        out_specs=pl.BlockSpec((tm, tn), lambda i, j: (i, j)),
        out_shape=jax.ShapeDtypeStruct((t, d), F32),
        scratch_shapes=[pltpu.VMEM((tm + _POOL_HIST, d), F32), pltpu.VMEM((tm, d), BF16)],
        compiler_params=_params(("parallel", "arbitrary"), 56),
        name="pool_mixer",
    )(u, u, x, w_grp, scale, w_out)


def _pad_heads(w, n, dh):
    d = w.shape[0]
    return jnp.pad(w.reshape(d, n, dh), ((0, 0), (0, 0), (0, HEAD_PAD - dh))).reshape(d, n * HEAD_PAD)


def _nsa_weights(w_in, rel_bias):
    g, hg = N_KV_GROUPS, HEADS_PER_GROUP
    qw, kw, vw = N_HEADS * D_K, g * D_K, g * D_V
    offs = np.cumsum([0, qw, kw, vw, kw, vw, kw, vw])
    wq, wkc, wvc, wks, wvs, wkw, wvw = [w_in[:, offs[i]:offs[i + 1]] for i in range(7)]
    wgt = w_in[:, offs[7]:]
    w_heads = jnp.concatenate([
        _pad_heads(wq, N_HEADS, D_K), _pad_heads(wks, g, D_K), _pad_heads(wkw, g, D_K),
        _pad_heads(wvs, g, D_V), _pad_heads(wvw, g, D_V)], axis=1).astype(BF16)
    scale = np.ones((N_SLOTS, HEAD_PAD), np.float32)
    scale[:N_HEADS] = D_K ** -0.5 * LOG2E
    add = np.zeros((N_SLOTS, HEAD_PAD), np.float32)
    add[SLOT_KS:SLOT_VS, D_K] = 1.0
    add[SLOT_VS:, D_V] = 1.0
    add = jnp.asarray(add).at[:N_HEADS, D_K].set(rel_bias[N_BUCKETS - 1].astype(F32) * LOG2E)
    wg = wgt.reshape(-1, 3, g, hg).transpose(0, 2, 1, 3).reshape(-1, g, 3 * hg)
    wg = jnp.pad(wg, ((0, 0), (0, 0), (0, LANES - 3 * hg))).reshape(-1, g * LANES)
    w_side = jnp.concatenate([wkc, wvc, wg], axis=1).astype(BF16)
    return w_heads, jnp.asarray(scale).reshape(1, -1), add.reshape(1, -1), w_side


def _overlap_matrix(nc, ns):
    i = np.arange(nc)[:, None]
    j = np.arange(ns)[None, :]
    ratio = SLC_BLOCK // CMP_STRIDE
    diff = i - ratio * j
    w = np.zeros((nc, ns), np.float32)
    for n in range(CMP_BLOCK // CMP_STRIDE):
        w += ((diff + n >= 0) & (diff + n < ratio)).astype(np.float32)
    w[nc - 1] = 0.0
    return jnp.asarray(w, dtype=BF16)


def _nsa_layer(x, gamma, rel_bias, w_in, pos_k, pos_v, ck_w1, ck_b1, ck_w2, cv_w1, cv_b1, cv_w2,
               w_out, batch):
    t = x.shape[0]
    seq = t // batch
    assert seq % Q_TILE == 0
    w_heads, sc_row, ad_row, w_side = _nsa_weights(w_in, rel_bias)
    tb, lc = _build_tables(rel_bias)
    qkv = _norm_heads(x, gamma, w_heads, sc_row, ad_row)
    kc, vc, gates = _norm_side(x, gamma, w_side)
    kcmp = _compress(kc, pos_k, ck_w1, ck_b1, ck_w2, D_K, D_K, batch)
    vcmp = _compress(vc, pos_v, cv_w1, cv_b1, cv_w2, D_V, None, batch)
    ov = _overlap_matrix(seq // CMP_STRIDE, seq // SLC_BLOCK)
    o_c, sel = _cmp_attention(qkv, lc, kcmp, vcmp, ov, gates, batch)
    o_s = _sel_attention(qkv, sel, tb, gates, batch)
    o_w = _win_attention(qkv, tb, gates, batch)
    return _out_proj(o_c, o_s, o_w, w_out.astype(BF16), x)


def kernel(x, norm_mix, norm_ffn, norm_final, rel_bias, nsa_w_in, nsa_pos_k, nsa_pos_v, nsa_ck_w1, nsa_ck_b1, nsa_ck_w2, nsa_cv_w1, nsa_cv_b1, nsa_cv_w2, nsa_w_out, pool_w_in, pool_w_grp, pool_scale, pool_w_out, ffn_w_gu, ffn_w_down):
    batch, seq, d = x.shape
    h = x.reshape(batch * seq, d)
    gfin = norm_final.reshape(1, d)
    h = _nsa_layer(h, norm_mix[0].reshape(1, d), rel_bias, nsa_w_in[0], nsa_pos_k[0], nsa_pos_v[0],
                   nsa_ck_w1[0], nsa_ck_b1[0], nsa_ck_w2[0], nsa_cv_w1[0], nsa_cv_b1[0], nsa_cv_w2[0],
                   nsa_w_out[0], batch)
    h = _ffn(h, norm_ffn[0].reshape(1, d), ffn_w_gu[0].astype(BF16), ffn_w_down[0].astype(BF16), gfin,
             final_norm=False)
    u = _norm_mm(h, norm_mix[1].reshape(1, d), pool_w_in[0].astype(BF16))
    h = _pool_mix(u, h, pool_w_grp[0].astype(BF16), pool_scale[0].reshape(1, d),
                  pool_w_out[0].astype(BF16), seq)
    h = _ffn(h, norm_ffn[1].reshape(1, d), ffn_w_gu[1].astype(BF16), ffn_w_down[1].astype(BF16), gfin,
             final_norm=True)
    return h.reshape(batch, seq, d)
```

```python
import functools
import math

import numpy as np
import jax
import jax.numpy as jnp
from jax import lax
from jax.experimental import pallas as pl
from jax.experimental.pallas import tpu as pltpu

D_MODEL = 2048
N_HEADS = 32
N_KV_GROUPS = 4
HEADS_PER_GROUP = N_HEADS // N_KV_GROUPS
D_K = 96
D_V = 64
CMP_BLOCK = 32
CMP_STRIDE = 16
CMP_HIDDEN = 256
SLC_BLOCK = 64
N_SLC = 16
WINDOW = 512
N_BUCKETS = 32
MAX_DISTANCE = 128
POOL_WINDOWS = (2, 4, 8, 16)
N_POOL_GROUPS = len(POOL_WINDOWS)
POOL_GROUP = D_MODEL // N_POOL_GROUPS
D_FF = -(-8 * D_MODEL // (3 * 256)) * 256
EPS = 1e-6
NEG_INF = -1e30
FORCE_SCORE = 1e9
LOG2E = math.log2(math.e)

LANES = 128
HEAD_PAD = LANES
Q_TILE = 256
KEY_CHUNK = 256
CMP_PER_TILE = Q_TILE // CMP_STRIDE
BIAS_K = CMP_PER_TILE * 8
SLOT_Q = 0
SLOT_KS = N_HEADS
SLOT_KW = SLOT_KS + N_KV_GROUPS
SLOT_VS = SLOT_KW + N_KV_GROUPS
SLOT_VW = SLOT_VS + N_KV_GROUPS
N_SLOTS = SLOT_VW + N_KV_GROUPS

F32 = jnp.float32
BF16 = jnp.bfloat16


def _bucket_np(dist):
    n = np.maximum(dist, 0)
    max_exact = N_BUCKETS // 2
    nf = np.maximum(n, 1).astype(np.float32)
    scaled = (np.log(nf / np.float32(max_exact)) / np.float32(math.log(MAX_DISTANCE / max_exact))
              * np.float32(N_BUCKETS - max_exact)).astype(np.float32)
    large = np.minimum(max_exact + scaled.astype(np.int32), N_BUCKETS - 1)
    return np.where(n < max_exact, n, large).astype(np.int32)


_FAR_DIST = int(np.max(np.nonzero(_bucket_np(np.arange(4096)) != N_BUCKETS - 1)[0])) + 1
assert _FAR_DIST <= CMP_STRIDE * 8 + (CMP_STRIDE - 1) - (CMP_BLOCK - 1) + 1 and _FAR_DIST <= KEY_CHUNK
_MASKED_BUCKET = N_BUCKETS
_ZERO_BUCKET = N_BUCKETS + 1


def _toeplitz_bucket_map():
    q = np.arange(Q_TILE)[:, None]
    r = np.arange(2 * KEY_CHUNK)[None, :]
    d = q + KEY_CHUNK - r
    b = _bucket_np(d)
    b = np.where(d >= _FAR_DIST, _ZERO_BUCKET, b)
    return np.where(d < 0, _MASKED_BUCKET, b).astype(np.int32)


def _cmp_bucket_map():
    q = np.arange(Q_TILE)[:, None]
    k = np.arange(BIAS_K)[None, :]
    a, b = q // CMP_STRIDE, q % CMP_STRIDE
    a2, m = k // 8, k % 8 + 1
    d = CMP_STRIDE * m + b - (CMP_BLOCK - 1)
    bk = _bucket_np(d)
    bk = np.where((d >= _FAR_DIST) | (d < 0) | (a != a2), _ZERO_BUCKET, bk)
    return bk.astype(np.int32)


def _params(sem, vmem_mb):
    return pltpu.CompilerParams(dimension_semantics=sem, vmem_limit_bytes=vmem_mb * 1024 * 1024)


def _rms(x, g):
    ms = jnp.mean(x * x, axis=-1, keepdims=True)
    return x * lax.rsqrt(ms + EPS) * g


def _dot(a, b):
    return jnp.dot(a, b, preferred_element_type=F32)


def _dot_nt(a, b):
    return lax.dot_general(a, b, (((1,), (1,)), ((), ())), preferred_element_type=F32)


def _tables_kernel(tbl_ref, bm_ref, bc_ref, tb_ref, lc_ref):
    h = pl.program_id(0) * HEADS_PER_GROUP + pl.program_id(1)
    far = tbl_ref[N_BUCKETS - 1, h]
    bm = bm_ref[...]
    bc = bc_ref[...]
    tb = jnp.where(bm == _MASKED_BUCKET, NEG_INF, 0.0).astype(F32)
    lc = jnp.zeros(bc.shape, F32)
    for b in range(N_BUCKETS - 1):
        val = (tbl_ref[b, h] - far) * LOG2E
        tb = jnp.where(bm == b, val, tb)
        lc = jnp.where(bc == b, val, lc)
    tb_ref[0, _TB_NEAR] = tb[:, :KEY_CHUNK]
    tb_ref[0, _TB_DIAG] = tb[:, KEY_CHUNK:]
    lc_ref[0] = lc.astype(BF16)


_TB_NEAR, _TB_DIAG = 0, 1
_TB_SECTIONS = 2


def _build_tables(rel_bias):
    bm = jnp.asarray(_toeplitz_bucket_map())
    bc = jnp.asarray(_cmp_bucket_map())
    g, hg = N_KV_GROUPS, HEADS_PER_GROUP
    return pl.pallas_call(
        _tables_kernel,
        grid=(g, hg),
        in_specs=[
            pl.BlockSpec(memory_space=pltpu.SMEM),
            pl.BlockSpec(bm.shape, lambda i, j: (0, 0)),
            pl.BlockSpec(bc.shape, lambda i, j: (0, 0)),
        ],
        out_specs=[
            pl.BlockSpec((1, _TB_SECTIONS, Q_TILE, KEY_CHUNK), lambda i, j: (i, 0, j, 0)),
            pl.BlockSpec((1, Q_TILE, BIAS_K), lambda i, j: (i, j, 0)),
        ],
        out_shape=[
            jax.ShapeDtypeStruct((g, _TB_SECTIONS, hg * Q_TILE, KEY_CHUNK), F32),
            jax.ShapeDtypeStruct((g, hg * Q_TILE, BIAS_K), BF16),
        ],
        compiler_params=_params(("arbitrary", "arbitrary"), 32),
        name="bias_tables",
    )(rel_bias, bm, bc)


def _norm_heads_kernel(x_ref, g_ref, w_ref, sc_ref, ad_ref, o_ref, h_scr, *, slabs):
    @pl.when(pl.program_id(1) == 0)
    def _():
        h_scr[...] = _rms(x_ref[...], g_ref[...]).astype(BF16)

    acc = _dot(h_scr[...], w_ref[...]) * sc_ref[...] + ad_ref[...]
    for k in range(slabs):
        o_ref[k] = acc[:, k * HEAD_PAD:(k + 1) * HEAD_PAD].astype(o_ref.dtype)


def _norm_heads(x, gamma, w, scale_row, add_row, *, tm=512, slabs=4):
    t, d = x.shape
    n = w.shape[1]
    tn = slabs * HEAD_PAD
    return pl.pallas_call(
        functools.partial(_norm_heads_kernel, slabs=slabs),
        grid=(t // tm, n // tn),
        in_specs=[
            pl.BlockSpec((tm, d), lambda i, j: (i, 0)),
            pl.BlockSpec((1, d), lambda i, j: (0, 0)),
            pl.BlockSpec((d, tn), lambda i, j: (0, j)),
            pl.BlockSpec((1, tn), lambda i, j: (0, j)),
            pl.BlockSpec((1, tn), lambda i, j: (0, j)),
        ],
        out_specs=pl.BlockSpec((slabs, tm, HEAD_PAD), lambda i, j: (j, i, 0)),
        out_shape=jax.ShapeDtypeStruct((n // HEAD_PAD, t, HEAD_PAD), BF16),
        scratch_shapes=[pltpu.VMEM((tm, d), BF16)],
        compiler_params=_params(("parallel", "arbitrary"), 48),
        name="norm_proj_heads",
    )(x, gamma, w, scale_row, add_row)


def _norm_side_kernel(x_ref, g_ref, w_ref, kc_ref, vc_ref, gt_ref, *, kw, vw):
    h = _rms(x_ref[...], g_ref[...]).astype(BF16)
    acc = _dot(h, w_ref[...])
    kc_ref[...] = acc[:, :kw]
    vc_ref[...] = acc[:, kw:kw + vw]
    for g in range(N_KV_GROUPS):
        lo = kw + vw + g * LANES
        gt_ref[g] = jax.nn.sigmoid(acc[:, lo:lo + LANES])


def _norm_side(x, gamma, w, *, tm=512):
    t, d = x.shape
    kw, vw = N_KV_GROUPS * D_K, N_KV_GROUPS * D_V
    n = w.shape[1]
    return pl.pallas_call(
        functools.partial(_norm_side_kernel, kw=kw, vw=vw),
        grid=(t // tm,),
        in_specs=[
            pl.BlockSpec((tm, d), lambda i: (i, 0)),
            pl.BlockSpec((1, d), lambda i: (0, 0)),
            pl.BlockSpec((d, n), lambda i: (0, 0)),
        ],
        out_specs=[
            pl.BlockSpec((tm, kw), lambda i: (i, 0)),
            pl.BlockSpec((tm, vw), lambda i: (i, 0)),
            pl.BlockSpec((N_KV_GROUPS, tm, LANES), lambda i: (0, i, 0)),
        ],
        out_shape=[
            jax.ShapeDtypeStruct((t, kw), F32),
            jax.ShapeDtypeStruct((t, vw), F32),
            jax.ShapeDtypeStruct((N_KV_GROUPS, t, LANES), F32),
        ],
        compiler_params=_params(("parallel",), 48),
        name="norm_proj_side",
    )(x, gamma, w)


def _norm_mm_kernel(x_ref, g_ref, w_ref, o_ref, h_scr):
    @pl.when(pl.program_id(1) == 0)
    def _():
        h_scr[...] = _rms(x_ref[...], g_ref[...]).astype(BF16)

    o_ref[...] = _dot(h_scr[...], w_ref[...])


def _norm_mm(x, gamma, w, *, tm=512, tn=1024):
    t, d = x.shape
    n = w.shape[1]
    return pl.pallas_call(
        _norm_mm_kernel,
        grid=(t // tm, n // tn),
        in_specs=[
            pl.BlockSpec((tm, d), lambda i, j: (i, 0)),
            pl.BlockSpec((1, d), lambda i, j: (0, 0)),
            pl.BlockSpec((d, tn), lambda i, j: (0, j)),
        ],
        out_specs=pl.BlockSpec((tm, tn), lambda i, j: (i, j)),
        out_shape=jax.ShapeDtypeStruct((t, n), F32),
        scratch_shapes=[pltpu.VMEM((tm, d), BF16)],
        compiler_params=_params(("parallel", "arbitrary"), 48),
        name="norm_matmul",
    )(x, gamma, w)


def _compress_kernel(x_ref, pa_ref, pb_ref, wa_ref, wb_ref, b1_ref, w2_ref, ad_ref, o_ref,
                     acca, accb, *, nk):
    k = pl.program_id(1)

    @pl.when(k == 0)
    def _():
        acca[...] = jnp.zeros_like(acca)
        accb[...] = jnp.zeros_like(accb)

    x = x_ref[...]
    acca[...] += _dot((x + pa_ref[...]).astype(BF16), wa_ref[...])
    accb[...] += _dot((x + pb_ref[...]).astype(BF16), wb_ref[...])

    @pl.when(k == nk - 1)
    def _():
        rows = accb.shape[0]
        hid = acca[...] + pltpu.roll(accb[...], rows - 1, 0) + b1_ref[...]
        hid = jax.nn.gelu(hid)
        out = _dot(hid.astype(BF16), w2_ref[...]) + ad_ref[...]
        for g in range(N_KV_GROUPS):
            o_ref[g] = out[:, g * HEAD_PAD:(g + 1) * HEAD_PAD].astype(o_ref.dtype)


def _compress(kv, pos, w1, b1, w2, dh, one_col, batch):
    t = kv.shape[0]
    g = N_KV_GROUPS
    chunks = t // CMP_STRIDE
    rows = chunks // batch
    width = CMP_STRIDE * g * dh
    x = kv.reshape(chunks, width)
    eye = jnp.eye(g, dtype=F32)
    w1r = w1.reshape(2, CMP_STRIDE, dh, CMP_HIDDEN)

    def expand(wh):
        return jnp.einsum('rdj,gh->rgdhj', wh, eye).reshape(width, g * CMP_HIDDEN).astype(BF16)

    wa, wb = expand(w1r[0]), expand(w1r[1])
    posr = pos.reshape(2, CMP_STRIDE, 1, dh)
    pa = jnp.broadcast_to(posr[0], (CMP_STRIDE, g, dh)).reshape(1, width)
    pb = jnp.broadcast_to(posr[1], (CMP_STRIDE, g, dh)).reshape(1, width)
    b1r = jnp.tile(b1.reshape(1, CMP_HIDDEN), (1, g))
    w2p = jnp.pad(w2, ((0, 0), (0, HEAD_PAD - dh)))
    w2x = jnp.einsum('jd,gh->gjhd', w2p, eye).reshape(g * CMP_HIDDEN, g * HEAD_PAD).astype(BF16)
    ad = np.zeros((1, g * HEAD_PAD), np.float32)
    if one_col is not None:
        ad[0, one_col::HEAD_PAD] = 1.0
    ad = jnp.asarray(ad)
    kt = width // 8
    nk = width // kt
    return pl.pallas_call(
        functools.partial(_compress_kernel, nk=nk),
        grid=(batch, nk),
        in_specs=[
            pl.BlockSpec((rows, kt), lambda b, k: (b, k)),
            pl.BlockSpec((1, kt), lambda b, k: (0, k)),
            pl.BlockSpec((1, kt), lambda b, k: (0, k)),
            pl.BlockSpec((kt, g * CMP_HIDDEN), lambda b, k: (k, 0)),
            pl.BlockSpec((kt, g * CMP_HIDDEN), lambda b, k: (k, 0)),
            pl.BlockSpec((1, g * CMP_HIDDEN), lambda b, k: (0, 0)),
            pl.BlockSpec((g * CMP_HIDDEN, g * HEAD_PAD), lambda b, k: (0, 0)),
            pl.BlockSpec((1, g * HEAD_PAD), lambda b, k: (0, 0)),
        ],
        out_specs=pl.BlockSpec((g, rows, HEAD_PAD), lambda b, k: (0, b, 0)),
        out_shape=jax.ShapeDtypeStruct((g, chunks, HEAD_PAD), BF16),
        scratch_shapes=[pltpu.VMEM((rows, g * CMP_HIDDEN), F32),
                        pltpu.VMEM((rows, g * CMP_HIDDEN), F32)],
        compiler_params=_params(("parallel", "arbitrary"), 48),
        name="compress_mlp",
    )(x, pa, pb, wa, wb, b1r, w2x, ad)


_CMP_MASK_LANE = D_K + 1
_CMP_THR_MIN = -((CMP_BLOCK - 1 + CMP_STRIDE - 1) // CMP_STRIDE)
_CMP_MASK_N = (Q_TILE - CMP_BLOCK) // CMP_STRIDE - _CMP_THR_MIN + 1
assert _CMP_MASK_LANE + _CMP_MASK_N <= HEAD_PAD


def _cmp_query_mask():
    q = np.arange(Q_TILE)
    thr = (q - (CMP_BLOCK - 1)) // CMP_STRIDE
    m = np.zeros((Q_TILE, HEAD_PAD), np.float32)
    m[q, _CMP_MASK_LANE + thr - _CMP_THR_MIN] = 1.0
    return jnp.asarray(np.tile(m, (HEADS_PER_GROUP, 1)), dtype=BF16)


def _cmp_kernel(q_ref, qm_ref, lc_ref, kc_ref, vc_ref, ovt_ref, gt_ref, o_ref, sel_ref,
                s_scr, p_scr, psum_scr, imp_scr, *, n_sel):
    tq = Q_TILE
    t = pl.program_id(2)
    t0 = t * tq
    nc = kc_ref.shape[1]
    ns = ovt_ref.shape[0]
    ii = lax.broadcasted_iota(jnp.int32, (nc, LANES), 0) - CMP_PER_TILE * t
    ll = lax.broadcasted_iota(jnp.int32, (nc, LANES), 1)
    hot = ii == ((ll >> 3) - ((ll & 7) + 1))
    late = ((ll >= _CMP_MASK_LANE) & (ll < _CMP_MASK_LANE + _CMP_MASK_N)
            & (ii > ll - (_CMP_MASK_LANE - _CMP_THR_MIN)))
    keys = jnp.where(late, NEG_INF, kc_ref[0].astype(F32)).astype(BF16)
    rhs = jnp.concatenate([keys, jnp.where(hot, 1.0, 0.0).astype(BF16)], axis=1)
    q = q_ref[...].reshape(HEADS_PER_GROUP * tq, HEAD_PAD) + qm_ref[...]
    lhs = jnp.concatenate([q, lc_ref[0]], axis=1)
    gt = gt_ref[0]
    n_chunk = max(nc // KEY_CHUNK, 1)
    wchunk = nc // n_chunk
    live = jnp.minimum((t0 + tq - CMP_BLOCK) // (CMP_STRIDE * wchunk) + 1, n_chunk)

    def attend(w):
        s_scr[:, :w] = _dot_nt(lhs, rhs[:w])
        for hh in range(HEADS_PER_GROUP):
            rows = slice(hh * tq, (hh + 1) * tq)
            s = s_scr[rows, :w]
            m = jnp.max(s, axis=-1, keepdims=True)
            e = jnp.exp2(s - m)
            l = jnp.sum(e, axis=-1, keepdims=True)
            p = e * jnp.where(m > 0.5 * NEG_INF, 1.0 / l, 0.0)
            if hh == 0:
                psum_scr[:, :w] = p
            else:
                psum_scr[:, :w] += p
            p_scr[rows, :w] = p.astype(BF16)
        o = _dot(p_scr[:, :w], vc_ref[0, :w, :])
        for hh in range(HEADS_PER_GROUP):
            o_ref[:, hh * D_V:(hh + 1) * D_V] = o[hh * tq:(hh + 1) * tq, :D_V] * gt[:, hh:hh + 1]
        psum = psum_scr[:, :w]
        p_hi = psum.astype(BF16)
        p_lo = (psum - p_hi.astype(F32)).astype(BF16)
        ovt = ovt_ref[:, :w]
        imp_scr[...] = _dot_nt(ovt, p_hi) + _dot_nt(ovt, p_lo)

    for k in range(1, n_chunk + 1):
        pl.when(live == k)(functools.partial(attend, k * wchunk))

    jr = lax.broadcasted_iota(jnp.int32, (ns, tq), 0)
    cur = (t0 + lax.broadcasted_iota(jnp.int32, (ns, tq), 1)) // SLC_BLOCK
    forced = (jr == 0) | (jr == cur) | (jr == cur - 1)
    n_forced = 3
    distinct = (t0 // SLC_BLOCK >= 2) & (n_sel > n_forced)
    imp = jnp.where(forced, jnp.where(distinct, -jnp.inf, FORCE_SCORE), imp_scr[...])
    imp = jnp.where(jr > cur, -1.0, imp)
    jrf = jr.astype(F32)

    def pick(_, imp):
        m = jnp.max(imp, axis=0, keepdims=True)
        first = jnp.min(jnp.where(imp == m, jrf, 1e9), axis=0, keepdims=True)
        return jnp.where(jrf == first, -jnp.inf, imp)

    imp = lax.fori_loop(0, n_sel - jnp.where(distinct, n_forced, 0), pick, imp)
    sel = jnp.where(imp == -jnp.inf, 1.0, 0.0).T.astype(BF16)
    if sel_ref.shape[2] == ns:
        sel_ref[0] = sel
    else:
        sel_ref[0] = jnp.zeros(sel_ref.shape[1:], BF16)
        sel_ref[0, :, 0:ns] = sel


def _cmp_attention(qkv, lc, kcmp, vcmp, ovt, gates, batch):
    t = qkv.shape[1]
    s = t // batch
    nc = kcmp.shape[1] // batch
    ns = ovt.shape[0]
    nt = s // Q_TILE
    g, hg = N_KV_GROUPS, HEADS_PER_GROUP
    ns_pad = -(-ns // LANES) * LANES
    return pl.pallas_call(
        functools.partial(_cmp_kernel, n_sel=min(N_SLC, ns)),
        grid=(batch, g, nt),
        in_specs=[
            pl.BlockSpec((hg, Q_TILE, HEAD_PAD), lambda b, i, j: (i, b * nt + j, 0)),
            pl.BlockSpec((hg * Q_TILE, HEAD_PAD), lambda b, i, j: (0, 0)),
            pl.BlockSpec((1, hg * Q_TILE, BIAS_K), lambda b, i, j: (i, 0, 0)),
            pl.BlockSpec((1, nc, HEAD_PAD), lambda b, i, j: (i, b, 0)),
            pl.BlockSpec((1, nc, HEAD_PAD), lambda b, i, j: (i, b, 0)),
            pl.BlockSpec((ns, nc), lambda b, i, j: (0, 0)),
            pl.BlockSpec((1, Q_TILE, LANES), lambda b, i, j: (i, b * nt + j, 0)),
        ],
        out_specs=[
            pl.BlockSpec((Q_TILE, hg * D_V), lambda b, i, j: (b * nt + j, i)),
            pl.BlockSpec((1, Q_TILE, ns_pad), lambda b, i, j: (i, b * nt + j, 0)),
        ],
        out_shape=[
            jax.ShapeDtypeStruct((t, N_HEADS * D_V), F32),
            jax.ShapeDtypeStruct((g, t, ns_pad), BF16),
        ],
        scratch_shapes=[pltpu.VMEM((hg * Q_TILE, nc), F32), pltpu.VMEM((hg * Q_TILE, nc), BF16),
                        pltpu.VMEM((Q_TILE, nc), F32), pltpu.VMEM((ns, Q_TILE), F32)],
        compiler_params=_params(("parallel", "parallel", "arbitrary"), 48),
        name="cmp_attention",
    )(qkv, _cmp_query_mask(), lc, kcmp, vcmp, ovt, gates)


def _sel_kernel(q_ref, ks_ref, vs_ref, sel_ref, tb_ref, gt_ref, o_ref,
                lhs_scr, s_scr, smax_scr, m_scr, acc_scr):
    tq = Q_TILE
    t = pl.program_id(2)
    n_var = sel_ref.shape[2] // LANES
    chunks_per_var = LANES * SLC_BLOCK // KEY_CHUNK
    blocks_per_chunk = KEY_CHUNK // SLC_BLOCK
    unsel = (1.0 - sel_ref[0].astype(F32)).astype(BF16)
    for var in range(n_var):
        @pl.when(t >= var * chunks_per_var)
        def _():
            for hh in range(HEADS_PER_GROUP):
                rows = slice(hh * tq, (hh + 1) * tq)
                lhs_scr[var, rows, 0:HEAD_PAD] = q_ref[hh]
                lhs_scr[var, rows, HEAD_PAD:] = unsel[:, var * LANES:(var + 1) * LANES]
    m_scr[...] = jnp.full(m_scr.shape, NEG_INF, F32)
    acc_scr[...] = jnp.zeros(acc_scr.shape, F32)
    lane = lax.broadcasted_iota(jnp.int32, (KEY_CHUNK, LANES), 1)
    kblk = lax.broadcasted_iota(jnp.int32, (KEY_CHUNK, LANES), 0) // SLC_BLOCK

    n_far = jnp.maximum(t - 1, 0)

    def qk(c, phantom):
        cc = jnp.minimum(c, t)
        start = pl.multiple_of(cc * KEY_CHUNK, KEY_CHUNK)
        k = ks_ref[0, pl.ds(start, KEY_CHUNK), :]
        flag = (cc % chunks_per_var) * blocks_per_chunk + kblk
        armed = lane == flag
        if phantom is not None:
            armed = armed | phantom
        rhs = jnp.concatenate([k, jnp.where(armed, NEG_INF, 0.0).astype(BF16)], axis=1)
        return _dot_nt(lhs_scr[cc // chunks_per_var], rhs)

    def accumulate(c, s, smax):
        cc = jnp.minimum(c, t)
        start = pl.multiple_of(cc * KEY_CHUNK, KEY_CHUNK)
        v = vs_ref[0, pl.ds(start, KEY_CHUNK), :]
        m_old = m_scr[...]
        m_new = jnp.maximum(m_old, smax)
        alpha = jnp.exp2(m_old - m_new)
        p = jnp.exp2(s - jnp.concatenate([m_new] * (KEY_CHUNK // LANES), axis=1))
        acc_scr[...] = alpha * acc_scr[...] + _dot(p.astype(BF16), v)
        m_scr[...] = m_new

    def scores(c, buf):
        s = qk(c, c >= n_far)
        s_scr[buf] = s
        smax_scr[buf] = jnp.broadcast_to(jnp.max(s, axis=-1, keepdims=True), smax_scr.shape[1:])

    def update(c, buf):
        accumulate(c, s_scr[buf], smax_scr[buf])

    def near(c, section):
        s = qk(c, None) + tb_ref[0, section]
        accumulate(c, s, jnp.max(s, axis=-1, keepdims=True))

    scores(0, 0)

    def trip(unroll):
        def body(_, base):
            for u in range(unroll):
                scores(base + u + 1, (u + 1) % 2)
                update(base + u, u % 2)
            return base + unroll
        return body

    done = lax.fori_loop(0, n_far // 4, trip(4), 0)
    lax.fori_loop(0, (n_far - done + 1) // 2, trip(2), done)

    @pl.when(t >= 1)
    def _():
        near(t - 1, _TB_NEAR)

    near(t, _TB_DIAG)
    gt = gt_ref[0]
    for hh in range(HEADS_PER_GROUP):
        rows = slice(hh * tq, (hh + 1) * tq)
        acc = acc_scr[rows]
        o = acc[:, :D_V] / acc[:, D_V:D_V + 1]
        o_ref[:, hh * D_V:(hh + 1) * D_V] = o * gt[:, HEADS_PER_GROUP + hh:HEADS_PER_GROUP + hh + 1]


def _sel_attention(qkv, sel, tb, gates, batch):
    t = qkv.shape[1]
    s = t // batch
    ns = sel.shape[2]
    nt = s // Q_TILE
    g, hg = N_KV_GROUPS, HEADS_PER_GROUP
    assert ns % LANES == 0 and HEAD_PAD == LANES
    return pl.pallas_call(
        _sel_kernel,
        grid=(batch, g, nt),
        in_specs=[
            pl.BlockSpec((hg, Q_TILE, HEAD_PAD), lambda b, i, j: (i, b * nt + j, 0)),
            pl.BlockSpec((1, s, HEAD_PAD), lambda b, i, j: (SLOT_KS + i, b, 0)),
            pl.BlockSpec((1, s, HEAD_PAD), lambda b, i, j: (SLOT_VS + i, b, 0)),
            pl.BlockSpec((1, Q_TILE, ns), lambda b, i, j: (i, b * nt + j, 0)),
            pl.BlockSpec((1, _TB_SECTIONS, hg * Q_TILE, KEY_CHUNK), lambda b, i, j: (i, 0, 0, 0)),
            pl.BlockSpec((1, Q_TILE, LANES), lambda b, i, j: (i, b * nt + j, 0)),
        ],
        out_specs=pl.BlockSpec((Q_TILE, hg * D_V), lambda b, i, j: (b * nt + j, i)),
        out_shape=jax.ShapeDtypeStruct((t, N_HEADS * D_V), F32),
        scratch_shapes=[pltpu.VMEM((ns // LANES, hg * Q_TILE, HEAD_PAD + LANES), BF16),
                        pltpu.VMEM((2, hg * Q_TILE, KEY_CHUNK), F32),
                        pltpu.VMEM((2, hg * Q_TILE, LANES), F32),
                        pltpu.VMEM((hg * Q_TILE, LANES), F32),
                        pltpu.VMEM((hg * Q_TILE, HEAD_PAD), F32)],
        compiler_params=_params(("parallel", "parallel", "arbitrary"), 60),
        name="sel_attention",
    )(qkv, qkv, qkv, sel, tb, gates)


def _win_kernel(q_ref, k0_ref, k1_ref, k2_ref, v0_ref, v1_ref, v2_ref, tb_ref, mw_ref, gt_ref, o_ref):
    tq = Q_TILE
    t = pl.program_id(2)
    pen0 = jnp.where(t >= 2, 0.0, NEG_INF).astype(F32)
    pen1 = jnp.where(t >= 1, 0.0, NEG_INF).astype(F32)
    q = q_ref[...].reshape(HEADS_PER_GROUP * tq, HEAD_PAD)
    s0 = _dot_nt(q, k0_ref[0]) + (mw_ref[...] + pen0)
    s1 = _dot_nt(q, k1_ref[0]) + (tb_ref[0, _TB_NEAR] + pen1)
    s2 = _dot_nt(q, k2_ref[0]) + tb_ref[0, _TB_DIAG]
    m = jnp.maximum(jnp.maximum(jnp.max(s0, axis=-1, keepdims=True),
                                jnp.max(s1, axis=-1, keepdims=True)),
                    jnp.max(s2, axis=-1, keepdims=True))
    acc = (_dot(jnp.exp2(s0 - m).astype(BF16), v0_ref[0]) + _dot(jnp.exp2(s1 - m).astype(BF16), v1_ref[0])
           + _dot(jnp.exp2(s2 - m).astype(BF16), v2_ref[0]))
    o = acc[:, :D_V] / acc[:, D_V:D_V + 1]
    gt = gt_ref[0]
    for hh in range(HEADS_PER_GROUP):
        col = 2 * HEADS_PER_GROUP + hh
        o_ref[:, hh * D_V:(hh + 1) * D_V] = o[hh * tq:(hh + 1) * tq] * gt[:, col:col + 1]


def _win_attention(qkv, tb, gates, batch):
    t = qkv.shape[1]
    s = t // batch
    nt = s // Q_TILE
    g, hg = N_KV_GROUPS, HEADS_PER_GROUP
    assert WINDOW == 2 * KEY_CHUNK and Q_TILE == KEY_CHUNK
    q = np.arange(Q_TILE)[:, None]
    r = np.arange(KEY_CHUNK)[None, :]
    mw = jnp.asarray(np.tile(np.where(r > q, 0.0, NEG_INF).astype(np.float32), (hg, 1)))

    def kv_spec(slot, back):
        return pl.BlockSpec((1, KEY_CHUNK, HEAD_PAD),
                            lambda b, i, j: (slot + i, b * nt + jnp.maximum(j - back, 0), 0))

    return pl.pallas_call(
        _win_kernel,
        grid=(batch, g, nt),
        in_specs=[
            pl.BlockSpec((hg, Q_TILE, HEAD_PAD), lambda b, i, j: (i, b * nt + j, 0)),
            kv_spec(SLOT_KW, 2), kv_spec(SLOT_KW, 1), kv_spec(SLOT_KW, 0),
            kv_spec(SLOT_VW, 2), kv_spec(SLOT_VW, 1), kv_spec(SLOT_VW, 0),
            pl.BlockSpec((1, 2, hg * Q_TILE, KEY_CHUNK), lambda b, i, j: (i, 0, 0, 0)),
            pl.BlockSpec((hg * Q_TILE, KEY_CHUNK), lambda b, i, j: (0, 0)),
            pl.BlockSpec((1, Q_TILE, LANES), lambda b, i, j: (i, b * nt + j, 0)),
        ],
        out_specs=pl.BlockSpec((Q_TILE, hg * D_V), lambda b, i, j: (b * nt + j, i)),
        out_shape=jax.ShapeDtypeStruct((t, N_HEADS * D_V), F32),
        compiler_params=_params(("parallel", "parallel", "arbitrary"), 48),
        name="win_attention",
    )(qkv, qkv, qkv, qkv, qkv, qkv, qkv, tb, mw, gates)


def _out_proj_kernel(a_ref, b_ref, c_ref, w_ref, x_ref, o_ref, s_scr):
    @pl.when(pl.program_id(1) == 0)
    def _():
        s_scr[...] = (a_ref[...] + b_ref[...] + c_ref[...]).astype(BF16)

    o_ref[...] = x_ref[...] + _dot(s_scr[...], w_ref[...])


def _out_proj(a, b, c, w, x, *, tm=256, tn=1024):
    t, d = a.shape
    n = w.shape[1]
    row = pl.BlockSpec((tm, d), lambda i, j: (i, 0))
    return pl.pallas_call(
        _out_proj_kernel,
        grid=(t // tm, n // tn),
        in_specs=[row, row, row,
                  pl.BlockSpec((d, tn), lambda i, j: (0, j)),
                  pl.BlockSpec((tm, tn), lambda i, j: (i, j))],
        out_specs=pl.BlockSpec((tm, tn), lambda i, j: (i, j)),
        out_shape=jax.ShapeDtypeStruct((t, n), F32),
        scratch_shapes=[pltpu.VMEM((tm, d), BF16)],
        compiler_params=_params(("parallel", "arbitrary"), 48),
        name="attn_out_proj",
    )(a, b, c, w, x)


def _ffn_kernel(x_ref, g_ref, wg_ref, wu_ref, wd_ref, gf_ref, o_ref, h_scr, acc_scr, *, nf, final_norm):
    f = pl.program_id(1)

    @pl.when(f == 0)
    def _():
        h_scr[...] = _rms(x_ref[...], g_ref[...]).astype(BF16)
        acc_scr[...] = jnp.zeros_like(acc_scr)

    h = h_scr[...]
    gate = _dot(h, wg_ref[...])
    up = _dot(h, wu_ref[...])
    act = (jax.nn.silu(gate) * up).astype(BF16)
    acc_scr[...] += _dot(act, wd_ref[...])

    @pl.when(f == nf - 1)
    def _():
        y = x_ref[...] + acc_scr[...]
        if final_norm:
            y = _rms(y, gf_ref[...])
        o_ref[...] = y


def _ffn(x, gamma, w_gu, w_down, gamma_final, *, final_norm, tm=512, tf=512):
    t, d = x.shape
    nf = D_FF // tf
    return pl.pallas_call(
        functools.partial(_ffn_kernel, nf=nf, final_norm=final_norm),
        grid=(t // tm, nf),
        in_specs=[
            pl.BlockSpec((tm, d), lambda i, f: (i, 0)),
            pl.BlockSpec((1, d), lambda i, f: (0, 0)),
            pl.BlockSpec((d, tf), lambda i, f: (0, f)),
            pl.BlockSpec((d, tf), lambda i, f: (0, nf + f)),
            pl.BlockSpec((tf, d), lambda i, f: (f, 0)),
            pl.BlockSpec((1, d), lambda i, f: (0, 0)),
        ],
        out_specs=pl.BlockSpec((tm, d), lambda i, f: (i, 0)),
        out_shape=jax.ShapeDtypeStruct((t, d), F32),
        scratch_shapes=[pltpu.VMEM((tm, d), BF16), pltpu.VMEM((tm, d), F32)],
        compiler_params=_params(("parallel", "arbitrary"), 56),
        name="swiglu_ffn",
    )(x, gamma, w_gu, w_gu, w_down, gamma_final)


_POOL_HIST = 16


def _pool_kernel(u_ref, up_ref, x_ref, wg_ref, sc_ref, wo_ref, o_ref, ext_scr, mix_scr, *, tm, seq):
    i = pl.program_id(0)

    @pl.when(pl.program_id(1) == 0)
    def _():
        t0 = (i * tm) % seq
        hist = up_ref[...]
        ext_scr[0:_POOL_HIST, :] = jnp.where(t0 == 0, jnp.zeros_like(hist), hist)
        ext_scr[_POOL_HIST:, :] = u_ref[...]
        tpos = t0 + lax.broadcasted_iota(jnp.int32, (tm, 1), 0)
        for gi, w in enumerate(POOL_WINDOWS):
            cols = slice(gi * POOL_GROUP, (gi + 1) * POOL_GROUP)
            tot = ext_scr[_POOL_HIST:, cols]
            for back in range(1, w):
                tot = tot + ext_scr[_POOL_HIST - back:_POOL_HIST - back + tm, cols]
            cnt = jnp.minimum(tpos + 1, w).astype(F32)
            pooled = tot / cnt - ext_scr[_POOL_HIST:, cols]
            mixed = _dot(pooled.astype(BF16), wg_ref[gi]) * sc_ref[:, cols]
            mix_scr[:, cols] = mixed.astype(BF16)

    o_ref[...] = x_ref[...] + _dot(mix_scr[...], wo_ref[...])


def _pool_mix(u, x, w_grp, scale, w_out, seq, *, tm=512, tn=1024):
    t, d = u.shape
    hb = tm // _POOL_HIST
    return pl.pallas_call(
        functools.partial(_pool_kernel, tm=tm, seq=seq),
        grid=(t // tm, d // tn),
        in_specs=[
            pl.BlockSpec((tm, d), lambda i, j: (i, 0)),
            pl.BlockSpec((_POOL_HIST, d), lambda i, j: (jnp.maximum(i * hb - 1, 0), 0)),
            pl.BlockSpec((tm, tn), lambda i, j: (i, j)),
            pl.BlockSpec((N_POOL_GROUPS, POOL_GROUP, POOL_GROUP), lambda i, j: (0, 0, 0)),
            pl.BlockSpec((1, d), lambda i, j: (0, 0)),
            pl.BlockSpec((d, tn), lambda i, j: (0, j)),
        ],
        out_specs=pl.BlockSpec((tm, tn), lambda i, j: (i, j)),
        out_shape=jax.ShapeDtypeStruct((t, d), F32),
        scratch_shapes=[pltpu.VMEM((tm + _POOL_HIST, d), F32), pltpu.VMEM((tm, d), BF16)],
        compiler_params=_params(("parallel", "arbitrary"), 56),
        name="pool_mixer",
    )(u, u, x, w_grp, scale, w_out)


def _pad_heads(w, n, dh):
    d = w.shape[0]
    return jnp.pad(w.reshape(d, n, dh), ((0, 0), (0, 0), (0, HEAD_PAD - dh))).reshape(d, n * HEAD_PAD)


def _nsa_weights(w_in, rel_bias):
    g, hg = N_KV_GROUPS, HEADS_PER_GROUP
    qw, kw, vw = N_HEADS * D_K, g * D_K, g * D_V
    offs = np.cumsum([0, qw, kw, vw, kw, vw, kw, vw])
    wq, wkc, wvc, wks, wvs, wkw, wvw = [w_in[:, offs[i]:offs[i + 1]] for i in range(7)]
    wgt = w_in[:, offs[7]:]
    w_heads = jnp.concatenate([
        _pad_heads(wq, N_HEADS, D_K), _pad_heads(wks, g, D_K), _pad_heads(wkw, g, D_K),
        _pad_heads(wvs, g, D_V), _pad_heads(wvw, g, D_V)], axis=1).astype(BF16)
    scale = np.ones((N_SLOTS, HEAD_PAD), np.float32)
    scale[:N_HEADS] = D_K ** -0.5 * LOG2E
    add = np.zeros((N_SLOTS, HEAD_PAD), np.float32)
    add[SLOT_KS:SLOT_VS, D_K] = 1.0
    add[SLOT_VS:, D_V] = 1.0
    add = jnp.asarray(add).at[:N_HEADS, D_K].set(rel_bias[N_BUCKETS - 1].astype(F32) * LOG2E)
    wg = wgt.reshape(-1, 3, g, hg).transpose(0, 2, 1, 3).reshape(-1, g, 3 * hg)
    wg = jnp.pad(wg, ((0, 0), (0, 0), (0, LANES - 3 * hg))).reshape(-1, g * LANES)
    w_side = jnp.concatenate([wkc, wvc, wg], axis=1).astype(BF16)
    return w_heads, jnp.asarray(scale).reshape(1, -1), add.reshape(1, -1), w_side


def _overlap_matrix_t(nc, ns):
    i = np.arange(nc)[None, :]
    j = np.arange(ns)[:, None]
    ratio = SLC_BLOCK // CMP_STRIDE
    diff = i - ratio * j
    w = np.zeros((ns, nc), np.float32)
    for n in range(CMP_BLOCK // CMP_STRIDE):
        w += ((diff + n >= 0) & (diff + n < ratio)).astype(np.float32)
    w[:, nc - 1] = 0.0
    return jnp.asarray(w, dtype=BF16)


def _nsa_layer(x, gamma, rel_bias, w_in, pos_k, pos_v, ck_w1, ck_b1, ck_w2, cv_w1, cv_b1, cv_w2,
               w_out, batch):
    t = x.shape[0]
    seq = t // batch
    assert seq % Q_TILE == 0
    w_heads, sc_row, ad_row, w_side = _nsa_weights(w_in, rel_bias)
    tb, lc = _build_tables(rel_bias)
    qkv = _norm_heads(x, gamma, w_heads, sc_row, ad_row)
    kc, vc, gates = _norm_side(x, gamma, w_side)
    kcmp = _compress(kc, pos_k, ck_w1, ck_b1, ck_w2, D_K, D_K, batch)
    vcmp = _compress(vc, pos_v, cv_w1, cv_b1, cv_w2, D_V, None, batch)
    ovt = _overlap_matrix_t(seq // CMP_STRIDE, seq // SLC_BLOCK)
    o_c, sel = _cmp_attention(qkv, lc, kcmp, vcmp, ovt, gates, batch)
    o_s = _sel_attention(qkv, sel, tb, gates, batch)
    o_w = _win_attention(qkv, tb, gates, batch)
    return _out_proj(o_c, o_s, o_w, w_out.astype(BF16), x)


def kernel(x, norm_mix, norm_ffn, norm_final, rel_bias, nsa_w_in, nsa_pos_k, nsa_pos_v, nsa_ck_w1, nsa_ck_b1, nsa_ck_w2, nsa_cv_w1, nsa_cv_b1, nsa_cv_w2, nsa_w_out, pool_w_in, pool_w_grp, pool_scale, pool_w_out, ffn_w_gu, ffn_w_down):
    batch, seq, d = x.shape
    h = x.reshape(batch * seq, d)
    gfin = norm_final.reshape(1, d)
    h = _nsa_layer(h, norm_mix[0].reshape(1, d), rel_bias, nsa_w_in[0], nsa_pos_k[0], nsa_pos_v[0],
                   nsa_ck_w1[0], nsa_ck_b1[0], nsa_ck_w2[0], nsa_cv_w1[0], nsa_cv_b1[0], nsa_cv_w2[0],
                   nsa_w_out[0], batch)
    h = _ffn(h, norm_ffn[0].reshape(1, d), ffn_w_gu[0].astype(BF16), ffn_w_down[0].astype(BF16), gfin,
             final_norm=False)
    u = _norm_mm(h, norm_mix[1].reshape(1, d), pool_w_in[0].astype(BF16))
    h = _pool_mix(u, h, pool_w_grp[0].astype(BF16), pool_scale[0].reshape(1, d),
                  pool_w_out[0].astype(BF16), seq)
    h = _ffn(h, norm_ffn[1].reshape(1, d), ffn_w_gu[1].astype(BF16), ffn_w_down[1].astype(BF16), gfin,
             final_norm=True)
    return h.reshape(batch, seq, d)
```

```python
import functools
import math

import numpy as np
import jax
import jax.numpy as jnp
from jax import lax
from jax.experimental import pallas as pl
from jax.experimental.pallas import tpu as pltpu

D_MODEL = 2048
N_HEADS = 32
N_KV_GROUPS = 4
HEADS_PER_GROUP = N_HEADS // N_KV_GROUPS
D_K = 96
D_V = 64
CMP_BLOCK = 32
CMP_STRIDE = 16
CMP_HIDDEN = 256
SLC_BLOCK = 64
N_SLC = 16
WINDOW = 512
N_BUCKETS = 32
MAX_DISTANCE = 128
POOL_WINDOWS = (2, 4, 8, 16)
N_POOL_GROUPS = len(POOL_WINDOWS)
POOL_GROUP = D_MODEL // N_POOL_GROUPS
D_FF = -(-8 * D_MODEL // (3 * 256)) * 256
EPS = 1e-6
NEG_INF = -1e30
FORCE_SCORE = 1e9
LOG2E = math.log2(math.e)

LANES = 128
HEAD_PAD = LANES
Q_TILE = 256
KEY_CHUNK = 256
CMP_PER_TILE = Q_TILE // CMP_STRIDE
BIAS_K = CMP_PER_TILE * 8
SLOT_Q = 0
SLOT_KS = N_HEADS
SLOT_KW = SLOT_KS + N_KV_GROUPS
SLOT_VS = SLOT_KW + N_KV_GROUPS
SLOT_VW = SLOT_VS + N_KV_GROUPS
N_SLOTS = SLOT_VW + N_KV_GROUPS

F32 = jnp.float32
BF16 = jnp.bfloat16


def _bucket_np(dist):
    n = np.maximum(dist, 0)
    max_exact = N_BUCKETS // 2
    nf = np.maximum(n, 1).astype(np.float32)
    scaled = (np.log(nf / np.float32(max_exact)) / np.float32(math.log(MAX_DISTANCE / max_exact))
              * np.float32(N_BUCKETS - max_exact)).astype(np.float32)
    large = np.minimum(max_exact + scaled.astype(np.int32), N_BUCKETS - 1)
    return np.where(n < max_exact, n, large).astype(np.int32)


_FAR_DIST = int(np.max(np.nonzero(_bucket_np(np.arange(4096)) != N_BUCKETS - 1)[0])) + 1
assert _FAR_DIST <= CMP_STRIDE * 8 + (CMP_STRIDE - 1) - (CMP_BLOCK - 1) + 1 and _FAR_DIST <= KEY_CHUNK
_MASKED_BUCKET = N_BUCKETS
_ZERO_BUCKET = N_BUCKETS + 1


def _toeplitz_bucket_map():
    q = np.arange(Q_TILE)[:, None]
    r = np.arange(2 * KEY_CHUNK)[None, :]
    d = q + KEY_CHUNK - r
    b = _bucket_np(d)
    b = np.where(d >= _FAR_DIST, _ZERO_BUCKET, b)
    return np.where(d < 0, _MASKED_BUCKET, b).astype(np.int32)


def _cmp_bucket_map():
    q = np.arange(Q_TILE)[:, None]
    k = np.arange(BIAS_K)[None, :]
    a, b = q // CMP_STRIDE, q % CMP_STRIDE
    a2, m = k // 8, k % 8 + 1
    d = CMP_STRIDE * m + b - (CMP_BLOCK - 1)
    bk = _bucket_np(d)
    bk = np.where((d >= _FAR_DIST) | (d < 0) | (a != a2), _ZERO_BUCKET, bk)
    return bk.astype(np.int32)


def _params(sem, vmem_mb):
    return pltpu.CompilerParams(dimension_semantics=sem, vmem_limit_bytes=vmem_mb * 1024 * 1024)


def _rms(x, g):
    ms = jnp.mean(x * x, axis=-1, keepdims=True)
    return x * lax.rsqrt(ms + EPS) * g


def _dot(a, b):
    return jnp.dot(a, b, preferred_element_type=F32)


def _dot_nt(a, b):
    return lax.dot_general(a, b, (((1,), (1,)), ((), ())), preferred_element_type=F32)


def _tables_kernel(tbl_ref, bm_ref, bc_ref, tb_ref, lc_ref):
    h = pl.program_id(0) * HEADS_PER_GROUP + pl.program_id(1)
    far = tbl_ref[N_BUCKETS - 1, h]
    bm = bm_ref[...]
    bc = bc_ref[...]
    tb = jnp.where(bm == _MASKED_BUCKET, NEG_INF, 0.0).astype(F32)
    lc = jnp.zeros(bc.shape, F32)
    for b in range(N_BUCKETS - 1):
        val = (tbl_ref[b, h] - far) * LOG2E
        tb = jnp.where(bm == b, val, tb)
        lc = jnp.where(bc == b, val, lc)
    tb_ref[0, _TB_NEAR] = tb[:, :KEY_CHUNK]
    tb_ref[0, _TB_DIAG] = tb[:, KEY_CHUNK:]
    lc_ref[0] = lc.astype(BF16)


_TB_NEAR, _TB_DIAG = 0, 1
_TB_SECTIONS = 2


def _build_tables(rel_bias):
    bm = jnp.asarray(_toeplitz_bucket_map())
    bc = jnp.asarray(_cmp_bucket_map())
    g, hg = N_KV_GROUPS, HEADS_PER_GROUP
    return pl.pallas_call(
        _tables_kernel,
        grid=(g, hg),
        in_specs=[
            pl.BlockSpec(memory_space=pltpu.SMEM),
            pl.BlockSpec(bm.shape, lambda i, j: (0, 0)),
            pl.BlockSpec(bc.shape, lambda i, j: (0, 0)),
        ],
        out_specs=[
            pl.BlockSpec((1, _TB_SECTIONS, Q_TILE, KEY_CHUNK), lambda i, j: (i, 0, j, 0)),
            pl.BlockSpec((1, Q_TILE, BIAS_K), lambda i, j: (i, j, 0)),
        ],
        out_shape=[
            jax.ShapeDtypeStruct((g, _TB_SECTIONS, hg * Q_TILE, KEY_CHUNK), F32),
            jax.ShapeDtypeStruct((g, hg * Q_TILE, BIAS_K), BF16),
        ],
        compiler_params=_params(("arbitrary", "arbitrary"), 32),
        name="bias_tables",
    )(rel_bias, bm, bc)


def _norm_heads_kernel(x_ref, g_ref, w_ref, sc_ref, ad_ref, o_ref, h_scr, *, slabs):
    @pl.when(pl.program_id(1) == 0)
    def _():
        h_scr[...] = _rms(x_ref[...], g_ref[...]).astype(BF16)

    acc = _dot(h_scr[...], w_ref[...]) * sc_ref[...] + ad_ref[...]
    for k in range(slabs):
        o_ref[k] = acc[:, k * HEAD_PAD:(k + 1) * HEAD_PAD].astype(o_ref.dtype)


def _norm_heads(x, gamma, w, scale_row, add_row, *, tm=512, slabs=8):
    t, d = x.shape
    n = w.shape[1]
    tn = slabs * HEAD_PAD
    return pl.pallas_call(
        functools.partial(_norm_heads_kernel, slabs=slabs),
        grid=(t // tm, n // tn),
        in_specs=[
            pl.BlockSpec((tm, d), lambda i, j: (i, 0)),
            pl.BlockSpec((1, d), lambda i, j: (0, 0)),
            pl.BlockSpec((d, tn), lambda i, j: (0, j)),
            pl.BlockSpec((1, tn), lambda i, j: (0, j)),
            pl.BlockSpec((1, tn), lambda i, j: (0, j)),
        ],
        out_specs=pl.BlockSpec((slabs, tm, HEAD_PAD), lambda i, j: (j, i, 0)),
        out_shape=jax.ShapeDtypeStruct((n // HEAD_PAD, t, HEAD_PAD), BF16),
        scratch_shapes=[pltpu.VMEM((tm, d), BF16)],
        compiler_params=_params(("parallel", "arbitrary"), 48),
        name="norm_proj_heads",
    )(x, gamma, w, scale_row, add_row)


def _norm_side_kernel(x_ref, g_ref, w_ref, kc_ref, vc_ref, gt_ref, *, kw, vw):
    h = _rms(x_ref[...], g_ref[...]).astype(BF16)
    acc = _dot(h, w_ref[...])
    kc_ref[...] = acc[:, :kw]
    vc_ref[...] = acc[:, kw:kw + vw]
    for g in range(N_KV_GROUPS):
        lo = kw + vw + g * LANES
        gt_ref[g] = jax.nn.sigmoid(acc[:, lo:lo + LANES])


def _norm_side(x, gamma, w, *, tm=512):
    t, d = x.shape
    kw, vw = N_KV_GROUPS * D_K, N_KV_GROUPS * D_V
    n = w.shape[1]
    return pl.pallas_call(
        functools.partial(_norm_side_kernel, kw=kw, vw=vw),
        grid=(t // tm,),
        in_specs=[
            pl.BlockSpec((tm, d), lambda i: (i, 0)),
            pl.BlockSpec((1, d), lambda i: (0, 0)),
            pl.BlockSpec((d, n), lambda i: (0, 0)),
        ],
        out_specs=[
            pl.BlockSpec((tm, kw), lambda i: (i, 0)),
            pl.BlockSpec((tm, vw), lambda i: (i, 0)),
            pl.BlockSpec((N_KV_GROUPS, tm, LANES), lambda i: (0, i, 0)),
        ],
        out_shape=[
            jax.ShapeDtypeStruct((t, kw), F32),
            jax.ShapeDtypeStruct((t, vw), F32),
            jax.ShapeDtypeStruct((N_KV_GROUPS, t, LANES), F32),
        ],
        compiler_params=_params(("parallel",), 48),
        name="norm_proj_side",
    )(x, gamma, w)


def _norm_mm_kernel(x_ref, g_ref, w_ref, o_ref, h_scr):
    @pl.when(pl.program_id(1) == 0)
    def _():
        h_scr[...] = _rms(x_ref[...], g_ref[...]).astype(BF16)

    o_ref[...] = _dot(h_scr[...], w_ref[...])


def _norm_mm(x, gamma, w, *, tm=512, tn=1024):
    t, d = x.shape
    n = w.shape[1]
    return pl.pallas_call(
        _norm_mm_kernel,
        grid=(t // tm, n // tn),
        in_specs=[
            pl.BlockSpec((tm, d), lambda i, j: (i, 0)),
            pl.BlockSpec((1, d), lambda i, j: (0, 0)),
            pl.BlockSpec((d, tn), lambda i, j: (0, j)),
        ],
        out_specs=pl.BlockSpec((tm, tn), lambda i, j: (i, j)),
        out_shape=jax.ShapeDtypeStruct((t, n), F32),
        scratch_shapes=[pltpu.VMEM((tm, d), BF16)],
        compiler_params=_params(("parallel", "arbitrary"), 48),
        name="norm_matmul",
    )(x, gamma, w)


def _compress_kernel(x_ref, pa_ref, pb_ref, wa_ref, wb_ref, b1_ref, w2_ref, ad_ref, o_ref,
                     acca, accb, *, nk):
    k = pl.program_id(1)

    @pl.when(k == 0)
    def _():
        acca[...] = jnp.zeros_like(acca)
        accb[...] = jnp.zeros_like(accb)

    x = x_ref[...]
    acca[...] += _dot((x + pa_ref[...]).astype(BF16), wa_ref[...])
    accb[...] += _dot((x + pb_ref[...]).astype(BF16), wb_ref[...])

    @pl.when(k == nk - 1)
    def _():
        rows = accb.shape[0]
        hid = acca[...] + pltpu.roll(accb[...], rows - 1, 0) + b1_ref[...]
        hid = jax.nn.gelu(hid)
        out = _dot(hid.astype(BF16), w2_ref[...]) + ad_ref[...]
        for g in range(N_KV_GROUPS):
            o_ref[g] = out[:, g * HEAD_PAD:(g + 1) * HEAD_PAD].astype(o_ref.dtype)


def _compress(kv, pos, w1, b1, w2, dh, one_col, batch):
    t = kv.shape[0]
    g = N_KV_GROUPS
    chunks = t // CMP_STRIDE
    rows = chunks // batch
    width = CMP_STRIDE * g * dh
    x = kv.reshape(chunks, width)
    eye = jnp.eye(g, dtype=F32)
    w1r = w1.reshape(2, CMP_STRIDE, dh, CMP_HIDDEN)

    def expand(wh):
        return jnp.einsum('rdj,gh->rgdhj', wh, eye).reshape(width, g * CMP_HIDDEN).astype(BF16)

    wa, wb = expand(w1r[0]), expand(w1r[1])
    posr = pos.reshape(2, CMP_STRIDE, 1, dh)
    pa = jnp.broadcast_to(posr[0], (CMP_STRIDE, g, dh)).reshape(1, width)
    pb = jnp.broadcast_to(posr[1], (CMP_STRIDE, g, dh)).reshape(1, width)
    b1r = jnp.tile(b1.reshape(1, CMP_HIDDEN), (1, g))
    w2p = jnp.pad(w2, ((0, 0), (0, HEAD_PAD - dh)))
    w2x = jnp.einsum('jd,gh->gjhd', w2p, eye).reshape(g * CMP_HIDDEN, g * HEAD_PAD).astype(BF16)
    ad = np.zeros((1, g * HEAD_PAD), np.float32)
    if one_col is not None:
        ad[0, one_col::HEAD_PAD] = 1.0
    ad = jnp.asarray(ad)
    kt = width // 8
    nk = width // kt
    return pl.pallas_call(
        functools.partial(_compress_kernel, nk=nk),
        grid=(batch, nk),
        in_specs=[
            pl.BlockSpec((rows, kt), lambda b, k: (b, k)),
            pl.BlockSpec((1, kt), lambda b, k: (0, k)),
            pl.BlockSpec((1, kt), lambda b, k: (0, k)),
            pl.BlockSpec((kt, g * CMP_HIDDEN), lambda b, k: (k, 0)),
            pl.BlockSpec((kt, g * CMP_HIDDEN), lambda b, k: (k, 0)),
            pl.BlockSpec((1, g * CMP_HIDDEN), lambda b, k: (0, 0)),
            pl.BlockSpec((g * CMP_HIDDEN, g * HEAD_PAD), lambda b, k: (0, 0)),
            pl.BlockSpec((1, g * HEAD_PAD), lambda b, k: (0, 0)),
        ],
        out_specs=pl.BlockSpec((g, rows, HEAD_PAD), lambda b, k: (0, b, 0)),
        out_shape=jax.ShapeDtypeStruct((g, chunks, HEAD_PAD), BF16),
        scratch_shapes=[pltpu.VMEM((rows, g * CMP_HIDDEN), F32),
                        pltpu.VMEM((rows, g * CMP_HIDDEN), F32)],
        compiler_params=_params(("parallel", "arbitrary"), 48),
        name="compress_mlp",
    )(x, pa, pb, wa, wb, b1r, w2x, ad)


_CMP_MASK_LANE = D_K + 1
_CMP_THR_MIN = -((CMP_BLOCK - 1 + CMP_STRIDE - 1) // CMP_STRIDE)
_CMP_MASK_N = (Q_TILE - CMP_BLOCK) // CMP_STRIDE - _CMP_THR_MIN + 1
assert _CMP_MASK_LANE + _CMP_MASK_N <= HEAD_PAD


def _cmp_query_mask():
    q = np.arange(Q_TILE)
    thr = (q - (CMP_BLOCK - 1)) // CMP_STRIDE
    m = np.zeros((Q_TILE, HEAD_PAD), np.float32)
    m[q, _CMP_MASK_LANE + thr - _CMP_THR_MIN] = 1.0
    return jnp.asarray(np.tile(m, (HEADS_PER_GROUP, 1)), dtype=BF16)


def _cmp_kernel(q_ref, qm_ref, lc_ref, kc_ref, vc_ref, ovt_ref, gt_ref, o_ref, sel_ref,
                s_scr, p_scr, psum_scr, imp_scr, *, n_sel):
    tq = Q_TILE
    t = pl.program_id(2)
    t0 = t * tq
    nc = kc_ref.shape[1]
    ns = ovt_ref.shape[0]
    ii = lax.broadcasted_iota(jnp.int32, (nc, LANES), 0) - CMP_PER_TILE * t
    ll = lax.broadcasted_iota(jnp.int32, (nc, LANES), 1)
    hot = ii == ((ll >> 3) - ((ll & 7) + 1))
    late = ((ll >= _CMP_MASK_LANE) & (ll < _CMP_MASK_LANE + _CMP_MASK_N)
            & (ii > ll - (_CMP_MASK_LANE - _CMP_THR_MIN)))
    keys = jnp.where(late, NEG_INF, kc_ref[0].astype(F32)).astype(BF16)
    rhs = jnp.concatenate([keys, jnp.where(hot, 1.0, 0.0).astype(BF16)], axis=1)
    q = q_ref[...].reshape(HEADS_PER_GROUP * tq, HEAD_PAD) + qm_ref[...]
    lhs = jnp.concatenate([q, lc_ref[0]], axis=1)
    gt = gt_ref[0]
    n_chunk = max(nc // KEY_CHUNK, 1)
    wchunk = nc // n_chunk
    live = jnp.minimum((t0 + tq - CMP_BLOCK) // (CMP_STRIDE * wchunk) + 1, n_chunk)

    def attend(w):
        s_scr[:, :w] = _dot_nt(lhs, rhs[:w])
        for hh in range(HEADS_PER_GROUP):
            rows = slice(hh * tq, (hh + 1) * tq)
            s = s_scr[rows, :w]
            m = jnp.max(s, axis=-1, keepdims=True)
            e = jnp.exp2(s - m)
            l = jnp.sum(e, axis=-1, keepdims=True)
            p = e * jnp.where(m > 0.5 * NEG_INF, 1.0 / l, 0.0)
            if hh == 0:
                psum_scr[:, :w] = p
            else:
                psum_scr[:, :w] += p
            p_scr[rows, :w] = p.astype(BF16)
        o = _dot(p_scr[:, :w], vc_ref[0, :w, :])
        for hh in range(HEADS_PER_GROUP):
            o_ref[:, hh * D_V:(hh + 1) * D_V] = (o[hh * tq:(hh + 1) * tq, :D_V] * gt[:, hh:hh + 1]).astype(o_ref.dtype)
        psum = psum_scr[:, :w]
        p_hi = psum.astype(BF16)
        p_lo = (psum - p_hi.astype(F32)).astype(BF16)
        ovt = ovt_ref[:, :w]
        imp_scr[...] = _dot_nt(ovt, p_hi) + _dot_nt(ovt, p_lo)

    for k in range(1, n_chunk + 1):
        pl.when(live == k)(functools.partial(attend, k * wchunk))

    jr = lax.broadcasted_iota(jnp.int32, (ns, tq), 0)
    cur = (t0 + lax.broadcasted_iota(jnp.int32, (ns, tq), 1)) // SLC_BLOCK
    forced = (jr == 0) | (jr == cur) | (jr == cur - 1)
    n_forced = 3
    distinct = (t0 // SLC_BLOCK >= 2) & (n_sel > n_forced)
    imp = jnp.where(forced, jnp.where(distinct, -jnp.inf, FORCE_SCORE), imp_scr[...])
    imp = jnp.where(jr > cur, -1.0, imp)
    jrf = jr.astype(F32)

    def pick(_, imp):
        m = jnp.max(imp, axis=0, keepdims=True)
        first = jnp.min(jnp.where(imp == m, jrf, 1e9), axis=0, keepdims=True)
        return jnp.where(jrf == first, -jnp.inf, imp)

    imp = lax.fori_loop(0, n_sel - jnp.where(distinct, n_forced, 0), pick, imp)
    sel = jnp.where(imp == -jnp.inf, 1.0, 0.0).T.astype(BF16)
    if sel_ref.shape[2] == ns:
        sel_ref[0] = sel
    else:
        sel_ref[0] = jnp.zeros(sel_ref.shape[1:], BF16)
        sel_ref[0, :, 0:ns] = sel


def _cmp_attention(qkv, lc, kcmp, vcmp, ovt, gates, batch):
    t = qkv.shape[1]
    s = t // batch
    nc = kcmp.shape[1] // batch
    ns = ovt.shape[0]
    nt = s // Q_TILE
    g, hg = N_KV_GROUPS, HEADS_PER_GROUP
    ns_pad = -(-ns // LANES) * LANES
    return pl.pallas_call(
        functools.partial(_cmp_kernel, n_sel=min(N_SLC, ns)),
        grid=(batch, g, nt),
        in_specs=[
            pl.BlockSpec((hg, Q_TILE, HEAD_PAD), lambda b, i, j: (i, b * nt + j, 0)),
            pl.BlockSpec((hg * Q_TILE, HEAD_PAD), lambda b, i, j: (0, 0)),
            pl.BlockSpec((1, hg * Q_TILE, BIAS_K), lambda b, i, j: (i, 0, 0)),
            pl.BlockSpec((1, nc, HEAD_PAD), lambda b, i, j: (i, b, 0)),
            pl.BlockSpec((1, nc, HEAD_PAD), lambda b, i, j: (i, b, 0)),
            pl.BlockSpec((ns, nc), lambda b, i, j: (0, 0)),
            pl.BlockSpec((1, Q_TILE, LANES), lambda b, i, j: (i, b * nt + j, 0)),
        ],
        out_specs=[
            pl.BlockSpec((Q_TILE, hg * D_V), lambda b, i, j: (b * nt + j, i)),
            pl.BlockSpec((1, Q_TILE, ns_pad), lambda b, i, j: (i, b * nt + j, 0)),
        ],
        out_shape=[
            jax.ShapeDtypeStruct((t, N_HEADS * D_V), BF16),
            jax.ShapeDtypeStruct((g, t, ns_pad), BF16),
        ],
        scratch_shapes=[pltpu.VMEM((hg * Q_TILE, nc), F32), pltpu.VMEM((hg * Q_TILE, nc), BF16),
                        pltpu.VMEM((Q_TILE, nc), F32), pltpu.VMEM((ns, Q_TILE), F32)],
        compiler_params=_params(("parallel", "parallel", "arbitrary"), 48),
        name="cmp_attention",
    )(qkv, _cmp_query_mask(), lc, kcmp, vcmp, ovt, gates)


_FAR_FACTOR = 4


def _sel_kernel(q_ref, ks_ref, vs_ref, sel_ref, tb_ref, gt_ref, o_ref,
                lhs_scr, s_scr, smax_scr, m_scr, acc_scr):
    tq = Q_TILE
    t = pl.program_id(2)
    n_var = sel_ref.shape[2] // LANES
    chunks_per_var = LANES * SLC_BLOCK // KEY_CHUNK
    blocks_per_chunk = KEY_CHUNK // SLC_BLOCK
    unsel = (1.0 - sel_ref[0].astype(F32)).astype(BF16)
    for var in range(n_var):
        @pl.when(t >= var * chunks_per_var)
        def _():
            for hh in range(HEADS_PER_GROUP):
                rows = slice(hh * tq, (hh + 1) * tq)
                lhs_scr[var, rows, 0:HEAD_PAD] = q_ref[hh]
                lhs_scr[var, rows, HEAD_PAD:] = unsel[:, var * LANES:(var + 1) * LANES]
    m_scr[...] = jnp.full(m_scr.shape, NEG_INF, F32)
    acc_scr[...] = jnp.zeros(acc_scr.shape, F32)
    lane = lax.broadcasted_iota(jnp.int32, (KEY_CHUNK, LANES), 1)
    kblk = lax.broadcasted_iota(jnp.int32, (KEY_CHUNK, LANES), 0) // SLC_BLOCK

    n_far = jnp.maximum(t - 1, 0)
    n_wide = n_far // _FAR_FACTOR
    wide = _FAR_FACTOR * KEY_CHUNK
    lane_w = lax.broadcasted_iota(jnp.int32, (wide, LANES), 1)
    kblk_w = lax.broadcasted_iota(jnp.int32, (wide, LANES), 0) // SLC_BLOCK

    def qk(start, nkeys, first_block, phantom):
        k = ks_ref[0, pl.ds(start, nkeys), :]
        flag = first_block % LANES + (kblk if nkeys == KEY_CHUNK else kblk_w)
        armed = (lane if nkeys == KEY_CHUNK else lane_w) == flag
        if phantom is not None:
            armed = armed | phantom
        rhs = jnp.concatenate([k, jnp.where(armed, NEG_INF, 0.0).astype(BF16)], axis=1)
        return _dot_nt(lhs_scr[first_block // LANES], rhs)

    def accumulate(start, nkeys, s, smax):
        v = vs_ref[0, pl.ds(start, nkeys), :]
        m_old = m_scr[...]
        m_new = jnp.maximum(m_old, smax)
        alpha = jnp.exp2(m_old - m_new)
        p = jnp.exp2(s - jnp.concatenate([m_new] * (nkeys // LANES), axis=1))
        acc_scr[...] = alpha * acc_scr[...] + _dot(p.astype(BF16), v)
        m_scr[...] = m_new

    def wide_start(i):
        return pl.multiple_of(jnp.minimum(i, t // _FAR_FACTOR) * wide, wide)

    def scores(i, buf):
        ii = jnp.minimum(i, t // _FAR_FACTOR)
        s = qk(wide_start(i), wide, ii * (wide // SLC_BLOCK), i >= n_wide)
        s_scr[buf] = s
        smax_scr[buf] = jnp.broadcast_to(jnp.max(s, axis=-1, keepdims=True), smax_scr.shape[1:])

    def update(i, buf):
        accumulate(wide_start(i), wide, s_scr[buf], smax_scr[buf])

    def narrow(c, section):
        start = pl.multiple_of(c * KEY_CHUNK, KEY_CHUNK)
        s = qk(start, KEY_CHUNK, c * blocks_per_chunk, None)
        if section is not None:
            s = s + tb_ref[0, section]
        accumulate(start, KEY_CHUNK, s, jnp.max(s, axis=-1, keepdims=True))

    scores(0, 0)

    def trip(_, base):
        for u in range(2):
            scores(base + u + 1, (u + 1) % 2)
            update(base + u, u % 2)
        return base + 2

    lax.fori_loop(0, n_wide // 2, trip, 0)

    @pl.when(n_wide % 2 == 1)
    def _():
        update(n_wide - 1, 0)

    for r in range(_FAR_FACTOR - 1):
        @pl.when(n_far % _FAR_FACTOR > r)
        def _():
            narrow(n_wide * _FAR_FACTOR + r, None)

    @pl.when(t >= 1)
    def _():
        narrow(t - 1, _TB_NEAR)

    narrow(t, _TB_DIAG)
    gt = gt_ref[0]
    for hh in range(HEADS_PER_GROUP):
        rows = slice(hh * tq, (hh + 1) * tq)
        acc = acc_scr[rows]
        o = acc[:, :D_V] / acc[:, D_V:D_V + 1]
        o_ref[:, hh * D_V:(hh + 1) * D_V] = (o * gt[:, HEADS_PER_GROUP + hh:HEADS_PER_GROUP + hh + 1]).astype(o_ref.dtype)


def _sel_attention(qkv, sel, tb, gates, batch):
    t = qkv.shape[1]
    s = t // batch
    ns = sel.shape[2]
    nt = s // Q_TILE
    g, hg = N_KV_GROUPS, HEADS_PER_GROUP
    assert ns % LANES == 0 and HEAD_PAD == LANES
    assert s % (_FAR_FACTOR * KEY_CHUNK) == 0
    return pl.pallas_call(
        _sel_kernel,
        grid=(batch, g, nt),
        in_specs=[
            pl.BlockSpec((hg, Q_TILE, HEAD_PAD), lambda b, i, j: (i, b * nt + j, 0)),
            pl.BlockSpec((1, s, HEAD_PAD), lambda b, i, j: (SLOT_KS + i, b, 0)),
            pl.BlockSpec((1, s, HEAD_PAD), lambda b, i, j: (SLOT_VS + i, b, 0)),
            pl.BlockSpec((1, Q_TILE, ns), lambda b, i, j: (i, b * nt + j, 0)),
            pl.BlockSpec((1, _TB_SECTIONS, hg * Q_TILE, KEY_CHUNK), lambda b, i, j: (i, 0, 0, 0)),
            pl.BlockSpec((1, Q_TILE, LANES), lambda b, i, j: (i, b * nt + j, 0)),
        ],
        out_specs=pl.BlockSpec((Q_TILE, hg * D_V), lambda b, i, j: (b * nt + j, i)),
        out_shape=jax.ShapeDtypeStruct((t, N_HEADS * D_V), BF16),
        scratch_shapes=[pltpu.VMEM((ns // LANES, hg * Q_TILE, HEAD_PAD + LANES), BF16),
                        pltpu.VMEM((2, hg * Q_TILE, _FAR_FACTOR * KEY_CHUNK), F32),
                        pltpu.VMEM((2, hg * Q_TILE, LANES), F32),
                        pltpu.VMEM((hg * Q_TILE, LANES), F32),
                        pltpu.VMEM((hg * Q_TILE, HEAD_PAD), F32)],
        compiler_params=_params(("parallel", "parallel", "arbitrary"), 60),
        name="sel_attention",
    )(qkv, qkv, qkv, sel, tb, gates)


def _win_kernel(q_ref, k0_ref, k1_ref, k2_ref, v0_ref, v1_ref, v2_ref, tb_ref, mw_ref, gt_ref, o_ref):
    tq = Q_TILE
    t = pl.program_id(2)
    pen0 = jnp.where(t >= 2, 0.0, NEG_INF).astype(F32)
    pen1 = jnp.where(t >= 1, 0.0, NEG_INF).astype(F32)
    q = q_ref[...].reshape(HEADS_PER_GROUP * tq, HEAD_PAD)
    s0 = _dot_nt(q, k0_ref[0]) + (mw_ref[...] + pen0)
    s1 = _dot_nt(q, k1_ref[0]) + (tb_ref[0, _TB_NEAR] + pen1)
    s2 = _dot_nt(q, k2_ref[0]) + tb_ref[0, _TB_DIAG]
    m = jnp.maximum(jnp.maximum(jnp.max(s0, axis=-1, keepdims=True),
                                jnp.max(s1, axis=-1, keepdims=True)),
                    jnp.max(s2, axis=-1, keepdims=True))
    acc = (_dot(jnp.exp2(s0 - m).astype(BF16), v0_ref[0]) + _dot(jnp.exp2(s1 - m).astype(BF16), v1_ref[0])
           + _dot(jnp.exp2(s2 - m).astype(BF16), v2_ref[0]))
    o = acc[:, :D_V] / acc[:, D_V:D_V + 1]
    gt = gt_ref[0]
    for hh in range(HEADS_PER_GROUP):
        col = 2 * HEADS_PER_GROUP + hh
        o_ref[:, hh * D_V:(hh + 1) * D_V] = (o[hh * tq:(hh + 1) * tq] * gt[:, col:col + 1]).astype(o_ref.dtype)


def _win_attention(qkv, tb, gates, batch):
    t = qkv.shape[1]
    s = t // batch
    nt = s // Q_TILE
    g, hg = N_KV_GROUPS, HEADS_PER_GROUP
    assert WINDOW == 2 * KEY_CHUNK and Q_TILE == KEY_CHUNK
    q = np.arange(Q_TILE)[:, None]
    r = np.arange(KEY_CHUNK)[None, :]
    mw = jnp.asarray(np.tile(np.where(r > q, 0.0, NEG_INF).astype(np.float32), (hg, 1)))

    def kv_spec(slot, back):
        return pl.BlockSpec((1, KEY_CHUNK, HEAD_PAD),
                            lambda b, i, j: (slot + i, b * nt + jnp.maximum(j - back, 0), 0))

    return pl.pallas_call(
        _win_kernel,
        grid=(batch, g, nt),
        in_specs=[
            pl.BlockSpec((hg, Q_TILE, HEAD_PAD), lambda b, i, j: (i, b * nt + j, 0)),
            kv_spec(SLOT_KW, 2), kv_spec(SLOT_KW, 1), kv_spec(SLOT_KW, 0),
            kv_spec(SLOT_VW, 2), kv_spec(SLOT_VW, 1), kv_spec(SLOT_VW, 0),
            pl.BlockSpec((1, 2, hg * Q_TILE, KEY_CHUNK), lambda b, i, j: (i, 0, 0, 0)),
            pl.BlockSpec((hg * Q_TILE, KEY_CHUNK), lambda b, i, j: (0, 0)),
            pl.BlockSpec((1, Q_TILE, LANES), lambda b, i, j: (i, b * nt + j, 0)),
        ],
        out_specs=pl.BlockSpec((Q_TILE, hg * D_V), lambda b, i, j: (b * nt + j, i)),
        out_shape=jax.ShapeDtypeStruct((t, N_HEADS * D_V), BF16),
        compiler_params=_params(("parallel", "parallel", "arbitrary"), 48),
        name="win_attention",
    )(qkv, qkv, qkv, qkv, qkv, qkv, qkv, tb, mw, gates)


def _out_proj_kernel(a_ref, b_ref, c_ref, w_ref, x_ref, o_ref, s_scr):
    @pl.when(pl.program_id(1) == 0)
    def _():
        s_scr[...] = (a_ref[...].astype(F32) + b_ref[...].astype(F32) + c_ref[...].astype(F32)).astype(BF16)

    o_ref[...] = x_ref[...] + _dot(s_scr[...], w_ref[...])


def _out_proj(a, b, c, w, x, *, tm=512, tn=1024):
    t, d = a.shape
    n = w.shape[1]
    row = pl.BlockSpec((tm, d), lambda i, j: (i, 0))
    return pl.pallas_call(
        _out_proj_kernel,
        grid=(t // tm, n // tn),
        in_specs=[row, row, row,
                  pl.BlockSpec((d, tn), lambda i, j: (0, j)),
                  pl.BlockSpec((tm, tn), lambda i, j: (i, j))],
        out_specs=pl.BlockSpec((tm, tn), lambda i, j: (i, j)),
        out_shape=jax.ShapeDtypeStruct((t, n), F32),
        scratch_shapes=[pltpu.VMEM((tm, d), BF16)],
        compiler_params=_params(("parallel", "arbitrary"), 48),
        name="attn_out_proj",
    )(a, b, c, w, x)


def _ffn_kernel(x_ref, g_ref, wg_ref, wu_ref, wd_ref, gf_ref, o_ref, h_scr, acc_scr, *, nf, final_norm):
    f = pl.program_id(1)

    @pl.when(f == 0)
    def _():
        h_scr[...] = _rms(x_ref[...], g_ref[...]).astype(BF16)
        acc_scr[...] = jnp.zeros_like(acc_scr)

    h = h_scr[...]
    gate = _dot(h, wg_ref[...])
    up = _dot(h, wu_ref[...])
    act = (jax.nn.silu(gate) * up).astype(BF16)
    acc_scr[...] += _dot(act, wd_ref[...])

    @pl.when(f == nf - 1)
    def _():
        y = x_ref[...] + acc_scr[...]
        if final_norm:
            y = _rms(y, gf_ref[...])
        o_ref[...] = y


def _ffn(x, gamma, w_gu, w_down, gamma_final, *, final_norm, tm=512, tf=512):
    t, d = x.shape
    nf = D_FF // tf
    return pl.pallas_call(
        functools.partial(_ffn_kernel, nf=nf, final_norm=final_norm),
        grid=(t // tm, nf),
        in_specs=[
            pl.BlockSpec((tm, d), lambda i, f: (i, 0)),
            pl.BlockSpec((1, d), lambda i, f: (0, 0)),
            pl.BlockSpec((d, tf), lambda i, f: (0, f)),
            pl.BlockSpec((d, tf), lambda i, f: (0, nf + f)),
            pl.BlockSpec((tf, d), lambda i, f: (f, 0)),
            pl.BlockSpec((1, d), lambda i, f: (0, 0)),
        ],
        out_specs=pl.BlockSpec((tm, d), lambda i, f: (i, 0)),
        out_shape=jax.ShapeDtypeStruct((t, d), F32),
        scratch_shapes=[pltpu.VMEM((tm, d), BF16), pltpu.VMEM((tm, d), F32)],
        compiler_params=_params(("parallel", "arbitrary"), 56),
        name="swiglu_ffn",
    )(x, gamma, w_gu, w_gu, w_down, gamma_final)


_POOL_HIST = 16


def _pool_kernel(u_ref, up_ref, x_ref, wg_ref, sc_ref, wo_ref, o_ref, ext_scr, mix_scr, *, tm, seq):
    i = pl.program_id(0)

    @pl.when(pl.program_id(1) == 0)
    def _():
        t0 = (i * tm) % seq
        hist = up_ref[...]
        ext_scr[0:_POOL_HIST, :] = jnp.where(t0 == 0, jnp.zeros_like(hist), hist)
        ext_scr[_POOL_HIST:, :] = u_ref[...]
        tpos = t0 + lax.broadcasted_iota(jnp.int32, (tm, 1), 0)
        for gi, w in enumerate(POOL_WINDOWS):
            cols = slice(gi * POOL_GROUP, (gi + 1) * POOL_GROUP)
            tot = ext_scr[_POOL_HIST:, cols]
            for back in range(1, w):
                tot = tot + ext_scr[_POOL_HIST - back:_POOL_HIST - back + tm, cols]
            cnt = jnp.minimum(tpos + 1, w).astype(F32)
            pooled = tot / cnt - ext_scr[_POOL_HIST:, cols]
            mixed = _dot(pooled.astype(BF16), wg_ref[gi]) * sc_ref[:, cols]
            mix_scr[:, cols] = mixed.astype(BF16)

    o_ref[...] = x_ref[...] + _dot(mix_scr[...], wo_ref[...])


def _pool_mix(u, x, w_grp, scale, w_out, seq, *, tm=512, tn=1024):
    t, d = u.shape
    hb = tm // _POOL_HIST
    return pl.pallas_call(
        functools.partial(_pool_kernel, tm=tm, seq=seq),
        grid=(t // tm, d // tn),
        in_specs=[
            pl.BlockSpec((tm, d), lambda i, j: (i, 0)),
            pl.BlockSpec((_POOL_HIST, d), lambda i, j: (jnp.maximum(i * hb - 1, 0), 0)),
            pl.BlockSpec((tm, tn), lambda i, j: (i, j)),
            pl.BlockSpec((N_POOL_GROUPS, POOL_GROUP, POOL_GROUP), lambda i, j: (0, 0, 0)),
            pl.BlockSpec((1, d), lambda i, j: (0, 0)),
            pl.BlockSpec((d, tn), lambda i, j: (0, j)),
        ],
        out_specs=pl.BlockSpec((tm, tn), lambda i, j: (i, j)),
        out_shape=jax.ShapeDtypeStruct((t, d), F32),
        scratch_shapes=[pltpu.VMEM((tm + _POOL_HIST, d), F32), pltpu.VMEM((tm, d), BF16)],
        compiler_params=_params(("parallel", "arbitrary"), 56),
        name="pool_mixer",
    )(u, u, x, w_grp, scale, w_out)


def _pad_heads(w, n, dh):
    d = w.shape[0]
    return jnp.pad(w.reshape(d, n, dh), ((0, 0), (0, 0), (0, HEAD_PAD - dh))).reshape(d, n * HEAD_PAD)


def _nsa_weights(w_in, rel_bias):
    g, hg = N_KV_GROUPS, HEADS_PER_GROUP
    qw, kw, vw = N_HEADS * D_K, g * D_K, g * D_V
    offs = np.cumsum([0, qw, kw, vw, kw, vw, kw, vw])
    wq, wkc, wvc, wks, wvs, wkw, wvw = [w_in[:, offs[i]:offs[i + 1]] for i in range(7)]
    wgt = w_in[:, offs[7]:]
    w_heads = jnp.concatenate([
        _pad_heads(wq, N_HEADS, D_K), _pad_heads(wks, g, D_K), _pad_heads(wkw, g, D_K),
        _pad_heads(wvs, g, D_V), _pad_heads(wvw, g, D_V)], axis=1).astype(BF16)
    scale = np.ones((N_SLOTS, HEAD_PAD), np.float32)
    scale[:N_HEADS] = D_K ** -0.5 * LOG2E
    add = np.zeros((N_SLOTS, HEAD_PAD), np.float32)
    add[SLOT_KS:SLOT_VS, D_K] = 1.0
    add[SLOT_VS:, D_V] = 1.0
    add = jnp.asarray(add).at[:N_HEADS, D_K].set(rel_bias[N_BUCKETS - 1].astype(F32) * LOG2E)
    wg = wgt.reshape(-1, 3, g, hg).transpose(0, 2, 1, 3).reshape(-1, g, 3 * hg)
    wg = jnp.pad(wg, ((0, 0), (0, 0), (0, LANES - 3 * hg))).reshape(-1, g * LANES)
    w_side = jnp.concatenate([wkc, wvc, wg], axis=1).astype(BF16)
    return w_heads, jnp.asarray(scale).reshape(1, -1), add.reshape(1, -1), w_side


def _overlap_matrix_t(nc, ns):
    i = np.arange(nc)[None, :]
    j = np.arange(ns)[:, None]
    ratio = SLC_BLOCK // CMP_STRIDE
    diff = i - ratio * j
    w = np.zeros((ns, nc), np.float32)
    for n in range(CMP_BLOCK // CMP_STRIDE):
        w += ((diff + n >= 0) & (diff + n < ratio)).astype(np.float32)
    w[:, nc - 1] = 0.0
    return jnp.asarray(w, dtype=BF16)


def _nsa_layer(x, gamma, rel_bias, w_in, pos_k, pos_v, ck_w1, ck_b1, ck_w2, cv_w1, cv_b1, cv_w2,
               w_out, batch):
    t = x.shape[0]
    seq = t // batch
    assert seq % Q_TILE == 0
    w_heads, sc_row, ad_row, w_side = _nsa_weights(w_in, rel_bias)
    tb, lc = _build_tables(rel_bias)
    qkv = _norm_heads(x, gamma, w_heads, sc_row, ad_row)
    kc, vc, gates = _norm_side(x, gamma, w_side)
    kcmp = _compress(kc, pos_k, ck_w1, ck_b1, ck_w2, D_K, D_K, batch)
    vcmp = _compress(vc, pos_v, cv_w1, cv_b1, cv_w2, D_V, None, batch)
    ovt = _overlap_matrix_t(seq // CMP_STRIDE, seq // SLC_BLOCK)
    o_c, sel = _cmp_attention(qkv, lc, kcmp, vcmp, ovt, gates, batch)
    o_s = _sel_attention(qkv, sel, tb, gates, batch)
    o_w = _win_attention(qkv, tb, gates, batch)
    return _out_proj(o_c, o_s, o_w, w_out.astype(BF16), x)


def kernel(x, norm_mix, norm_ffn, norm_final, rel_bias, nsa_w_in, nsa_pos_k, nsa_pos_v, nsa_ck_w1, nsa_ck_b1, nsa_ck_w2, nsa_cv_w1, nsa_cv_b1, nsa_cv_w2, nsa_w_out, pool_w_in, pool_w_grp, pool_scale, pool_w_out, ffn_w_gu, ffn_w_down):
    batch, seq, d = x.shape
    h = x.reshape(batch * seq, d)
    gfin = norm_final.reshape(1, d)
    h = _nsa_layer(h, norm_mix[0].reshape(1, d), rel_bias, nsa_w_in[0], nsa_pos_k[0], nsa_pos_v[0],
                   nsa_ck_w1[0], nsa_ck_b1[0], nsa_ck_w2[0], nsa_cv_w1[0], nsa_cv_b1[0], nsa_cv_w2[0],
                   nsa_w_out[0], batch)
    h = _ffn(h, norm_ffn[0].reshape(1, d), ffn_w_gu[0].astype(BF16), ffn_w_down[0].astype(BF16), gfin,
             final_norm=False)
    u = _norm_mm(h, norm_mix[1].reshape(1, d), pool_w_in[0].astype(BF16))
    h = _pool_mix(u, h, pool_w_grp[0].astype(BF16), pool_scale[0].reshape(1, d),
                  pool_w_out[0].astype(BF16), seq)
    h = _ffn(h, norm_ffn[1].reshape(1, d), ffn_w_gu[1].astype(BF16), ffn_w_down[1].astype(BF16), gfin,
             final_norm=True)
    return h.reshape(batch, seq, d)
```

```python
import functools
import math

import numpy as np
import jax
import jax.numpy as jnp
from jax import lax
from jax.experimental import pallas as pl
from jax.experimental.pallas import tpu as pltpu

D_MODEL = 2048
N_HEADS = 32
N_KV_GROUPS = 4
HEADS_PER_GROUP = N_HEADS // N_KV_GROUPS
D_K = 96
D_V = 64
CMP_BLOCK = 32
CMP_STRIDE = 16
CMP_HIDDEN = 256
SLC_BLOCK = 64
N_SLC = 16
WINDOW = 512
N_BUCKETS = 32
MAX_DISTANCE = 128
POOL_WINDOWS = (2, 4, 8, 16)
N_POOL_GROUPS = len(POOL_WINDOWS)
POOL_GROUP = D_MODEL // N_POOL_GROUPS
D_FF = -(-8 * D_MODEL // (3 * 256)) * 256
EPS = 1e-6
NEG_INF = -1e30
FORCE_SCORE = 1e9
LOG2E = math.log2(math.e)

LANES = 128
HEAD_PAD = LANES
Q_TILE = 256
KEY_CHUNK = 256
CMP_PER_TILE = Q_TILE // CMP_STRIDE
BIAS_K = CMP_PER_TILE * 8
SLOT_Q = 0
SLOT_KS = N_HEADS
SLOT_KW = SLOT_KS + N_KV_GROUPS
SLOT_VS = SLOT_KW + N_KV_GROUPS
SLOT_VW = SLOT_VS + N_KV_GROUPS
N_SLOTS = SLOT_VW + N_KV_GROUPS

F32 = jnp.float32
BF16 = jnp.bfloat16


def _bucket_np(dist):
    n = np.maximum(dist, 0)
    max_exact = N_BUCKETS // 2
    nf = np.maximum(n, 1).astype(np.float32)
    scaled = (np.log(nf / np.float32(max_exact)) / np.float32(math.log(MAX_DISTANCE / max_exact))
              * np.float32(N_BUCKETS - max_exact)).astype(np.float32)
    large = np.minimum(max_exact + scaled.astype(np.int32), N_BUCKETS - 1)
    return np.where(n < max_exact, n, large).astype(np.int32)


_FAR_DIST = int(np.max(np.nonzero(_bucket_np(np.arange(4096)) != N_BUCKETS - 1)[0])) + 1
assert _FAR_DIST <= CMP_STRIDE * 8 + (CMP_STRIDE - 1) - (CMP_BLOCK - 1) + 1 and _FAR_DIST <= KEY_CHUNK
_MASKED_BUCKET = N_BUCKETS
_ZERO_BUCKET = N_BUCKETS + 1


def _toeplitz_bucket_map():
    q = np.arange(Q_TILE)[:, None]
    r = np.arange(2 * KEY_CHUNK)[None, :]
    d = q + KEY_CHUNK - r
    b = _bucket_np(d)
    b = np.where(d >= _FAR_DIST, _ZERO_BUCKET, b)
    return np.where(d < 0, _MASKED_BUCKET, b).astype(np.int32)


def _cmp_bucket_map():
    q = np.arange(Q_TILE)[:, None]
    k = np.arange(BIAS_K)[None, :]
    a, b = q // CMP_STRIDE, q % CMP_STRIDE
    a2, m = k // 8, k % 8 + 1
    d = CMP_STRIDE * m + b - (CMP_BLOCK - 1)
    bk = _bucket_np(d)
    bk = np.where((d >= _FAR_DIST) | (d < 0) | (a != a2), _ZERO_BUCKET, bk)
    return bk.astype(np.int32)


def _params(sem, vmem_mb):
    return pltpu.CompilerParams(dimension_semantics=sem, vmem_limit_bytes=vmem_mb * 1024 * 1024)


def _rms(x, g):
    ms = jnp.mean(x * x, axis=-1, keepdims=True)
    return x * lax.rsqrt(ms + EPS) * g


def _dot(a, b):
    return jnp.dot(a, b, preferred_element_type=F32)


def _dot_nt(a, b):
    return lax.dot_general(a, b, (((1,), (1,)), ((), ())), preferred_element_type=F32)


def _tables_kernel(tbl_ref, bm_ref, bc_ref, tb_ref, lc_ref):
    h = pl.program_id(0) * HEADS_PER_GROUP + pl.program_id(1)
    far = tbl_ref[N_BUCKETS - 1, h]
    bm = bm_ref[...]
    bc = bc_ref[...]
    tb = jnp.where(bm == _MASKED_BUCKET, NEG_INF, 0.0).astype(F32)
    lc = jnp.zeros(bc.shape, F32)
    for b in range(N_BUCKETS - 1):
        val = (tbl_ref[b, h] - far) * LOG2E
        tb = jnp.where(bm == b, val, tb)
        lc = jnp.where(bc == b, val, lc)
    tb_ref[0, _TB_NEAR] = tb[:, :KEY_CHUNK]
    tb_ref[0, _TB_DIAG] = tb[:, KEY_CHUNK:]
    lc_ref[0] = lc.astype(BF16)


_TB_NEAR, _TB_DIAG = 0, 1
_TB_SECTIONS = 2


def _build_tables(rel_bias):
    bm = jnp.asarray(_toeplitz_bucket_map())
    bc = jnp.asarray(_cmp_bucket_map())
    g, hg = N_KV_GROUPS, HEADS_PER_GROUP
    return pl.pallas_call(
        _tables_kernel,
        grid=(g, hg),
        in_specs=[
            pl.BlockSpec(memory_space=pltpu.SMEM),
            pl.BlockSpec(bm.shape, lambda i, j: (0, 0)),
            pl.BlockSpec(bc.shape, lambda i, j: (0, 0)),
        ],
        out_specs=[
            pl.BlockSpec((1, _TB_SECTIONS, Q_TILE, KEY_CHUNK), lambda i, j: (i, 0, j, 0)),
            pl.BlockSpec((1, Q_TILE, BIAS_K), lambda i, j: (i, j, 0)),
        ],
        out_shape=[
            jax.ShapeDtypeStruct((g, _TB_SECTIONS, hg * Q_TILE, KEY_CHUNK), F32),
            jax.ShapeDtypeStruct((g, hg * Q_TILE, BIAS_K), BF16),
        ],
        compiler_params=_params(("arbitrary", "arbitrary"), 32),
        name="bias_tables",
    )(rel_bias, bm, bc)


def _norm_heads_kernel(x_ref, g_ref, w_ref, sc_ref, ad_ref, o_ref, h_scr, *, slabs):
    @pl.when(pl.program_id(1) == 0)
    def _():
        h_scr[...] = _rms(x_ref[...], g_ref[...]).astype(BF16)

    acc = _dot(h_scr[...], w_ref[...]) * sc_ref[...] + ad_ref[...]
    for k in range(slabs):
        o_ref[k] = acc[:, k * HEAD_PAD:(k + 1) * HEAD_PAD].astype(o_ref.dtype)


def _norm_heads(x, gamma, w, scale_row, add_row, *, tm=512, slabs=8):
    t, d = x.shape
    n = w.shape[1]
    tn = slabs * HEAD_PAD
    return pl.pallas_call(
        functools.partial(_norm_heads_kernel, slabs=slabs),
        grid=(t // tm, n // tn),
        in_specs=[
            pl.BlockSpec((tm, d), lambda i, j: (i, 0)),
            pl.BlockSpec((1, d), lambda i, j: (0, 0)),
            pl.BlockSpec((d, tn), lambda i, j: (0, j)),
            pl.BlockSpec((1, tn), lambda i, j: (0, j)),
            pl.BlockSpec((1, tn), lambda i, j: (0, j)),
        ],
        out_specs=pl.BlockSpec((slabs, tm, HEAD_PAD), lambda i, j: (j, i, 0)),
        out_shape=jax.ShapeDtypeStruct((n // HEAD_PAD, t, HEAD_PAD), BF16),
        scratch_shapes=[pltpu.VMEM((tm, d), BF16)],
        compiler_params=_params(("parallel", "arbitrary"), 48),
        name="norm_proj_heads",
    )(x, gamma, w, scale_row, add_row)


def _norm_side_kernel(x_ref, g_ref, w_ref, kc_ref, vc_ref, gt_ref, *, kw, vw):
    h = _rms(x_ref[...], g_ref[...]).astype(BF16)
    acc = _dot(h, w_ref[...])
    kc_ref[...] = acc[:, :kw]
    vc_ref[...] = acc[:, kw:kw + vw]
    for g in range(N_KV_GROUPS):
        lo = kw + vw + g * LANES
        gt_ref[g] = jax.nn.sigmoid(acc[:, lo:lo + LANES])


def _norm_side(x, gamma, w, *, tm=512):
    t, d = x.shape
    kw, vw = N_KV_GROUPS * D_K, N_KV_GROUPS * D_V
    n = w.shape[1]
    return pl.pallas_call(
        functools.partial(_norm_side_kernel, kw=kw, vw=vw),
        grid=(t // tm,),
        in_specs=[
            pl.BlockSpec((tm, d), lambda i: (i, 0)),
            pl.BlockSpec((1, d), lambda i: (0, 0)),
            pl.BlockSpec((d, n), lambda i: (0, 0)),
        ],
        out_specs=[
            pl.BlockSpec((tm, kw), lambda i: (i, 0)),
            pl.BlockSpec((tm, vw), lambda i: (i, 0)),
            pl.BlockSpec((N_KV_GROUPS, tm, LANES), lambda i: (0, i, 0)),
        ],
        out_shape=[
            jax.ShapeDtypeStruct((t, kw), F32),
            jax.ShapeDtypeStruct((t, vw), F32),
            jax.ShapeDtypeStruct((N_KV_GROUPS, t, LANES), F32),
        ],
        compiler_params=_params(("parallel",), 48),
        name="norm_proj_side",
    )(x, gamma, w)


def _norm_mm_kernel(x_ref, g_ref, w_ref, o_ref, h_scr):
    @pl.when(pl.program_id(1) == 0)
    def _():
        h_scr[...] = _rms(x_ref[...], g_ref[...]).astype(BF16)

    o_ref[...] = _dot(h_scr[...], w_ref[...])


def _norm_mm(x, gamma, w, *, tm=512, tn=1024):
    t, d = x.shape
    n = w.shape[1]
    return pl.pallas_call(
        _norm_mm_kernel,
        grid=(t // tm, n // tn),
        in_specs=[
            pl.BlockSpec((tm, d), lambda i, j: (i, 0)),
            pl.BlockSpec((1, d), lambda i, j: (0, 0)),
            pl.BlockSpec((d, tn), lambda i, j: (0, j)),
        ],
        out_specs=pl.BlockSpec((tm, tn), lambda i, j: (i, j)),
        out_shape=jax.ShapeDtypeStruct((t, n), F32),
        scratch_shapes=[pltpu.VMEM((tm, d), BF16)],
        compiler_params=_params(("parallel", "arbitrary"), 48),
        name="norm_matmul",
    )(x, gamma, w)


def _compress_kernel(x_ref, pa_ref, pb_ref, wa_ref, wb_ref, b1_ref, w2_ref, ad_ref, o_ref,
                     acca, accb, *, nk):
    k = pl.program_id(1)

    @pl.when(k == 0)
    def _():
        acca[...] = jnp.zeros_like(acca)
        accb[...] = jnp.zeros_like(accb)

    x = x_ref[...]
    acca[...] += _dot((x + pa_ref[...]).astype(BF16), wa_ref[...])
    accb[...] += _dot((x + pb_ref[...]).astype(BF16), wb_ref[...])

    @pl.when(k == nk - 1)
    def _():
        rows = accb.shape[0]
        hid = acca[...] + pltpu.roll(accb[...], rows - 1, 0) + b1_ref[...]
        hid = jax.nn.gelu(hid)
        out = _dot(hid.astype(BF16), w2_ref[...]) + ad_ref[...]
        for g in range(N_KV_GROUPS):
            o_ref[g] = out[:, g * HEAD_PAD:(g + 1) * HEAD_PAD].astype(o_ref.dtype)


def _compress(kv, pos, w1, b1, w2, dh, one_col, batch):
    t = kv.shape[0]
    g = N_KV_GROUPS
    chunks = t // CMP_STRIDE
    rows = chunks // batch
    width = CMP_STRIDE * g * dh
    x = kv.reshape(chunks, width)
    eye = jnp.eye(g, dtype=F32)
    w1r = w1.reshape(2, CMP_STRIDE, dh, CMP_HIDDEN)

    def expand(wh):
        return jnp.einsum('rdj,gh->rgdhj', wh, eye).reshape(width, g * CMP_HIDDEN).astype(BF16)

    wa, wb = expand(w1r[0]), expand(w1r[1])
    posr = pos.reshape(2, CMP_STRIDE, 1, dh)
    pa = jnp.broadcast_to(posr[0], (CMP_STRIDE, g, dh)).reshape(1, width)
    pb = jnp.broadcast_to(posr[1], (CMP_STRIDE, g, dh)).reshape(1, width)
    b1r = jnp.tile(b1.reshape(1, CMP_HIDDEN), (1, g))
    w2p = jnp.pad(w2, ((0, 0), (0, HEAD_PAD - dh)))
    w2x = jnp.einsum('jd,gh->gjhd', w2p, eye).reshape(g * CMP_HIDDEN, g * HEAD_PAD).astype(BF16)
    ad = np.zeros((1, g * HEAD_PAD), np.float32)
    if one_col is not None:
        ad[0, one_col::HEAD_PAD] = 1.0
    ad = jnp.asarray(ad)
    kt = width // 8
    nk = width // kt
    return pl.pallas_call(
        functools.partial(_compress_kernel, nk=nk),
        grid=(batch, nk),
        in_specs=[
            pl.BlockSpec((rows, kt), lambda b, k: (b, k)),
            pl.BlockSpec((1, kt), lambda b, k: (0, k)),
            pl.BlockSpec((1, kt), lambda b, k: (0, k)),
            pl.BlockSpec((kt, g * CMP_HIDDEN), lambda b, k: (k, 0)),
            pl.BlockSpec((kt, g * CMP_HIDDEN), lambda b, k: (k, 0)),
            pl.BlockSpec((1, g * CMP_HIDDEN), lambda b, k: (0, 0)),
            pl.BlockSpec((g * CMP_HIDDEN, g * HEAD_PAD), lambda b, k: (0, 0)),
            pl.BlockSpec((1, g * HEAD_PAD), lambda b, k: (0, 0)),
        ],
        out_specs=pl.BlockSpec((g, rows, HEAD_PAD), lambda b, k: (0, b, 0)),
        out_shape=jax.ShapeDtypeStruct((g, chunks, HEAD_PAD), BF16),
        scratch_shapes=[pltpu.VMEM((rows, g * CMP_HIDDEN), F32),
                        pltpu.VMEM((rows, g * CMP_HIDDEN), F32)],
        compiler_params=_params(("parallel", "arbitrary"), 48),
        name="compress_mlp",
    )(x, pa, pb, wa, wb, b1r, w2x, ad)


_CMP_MASK_LANE = D_K + 1
_CMP_THR_MIN = -((CMP_BLOCK - 1 + CMP_STRIDE - 1) // CMP_STRIDE)
_CMP_MASK_N = (Q_TILE - CMP_BLOCK) // CMP_STRIDE - _CMP_THR_MIN + 1
assert _CMP_MASK_LANE + _CMP_MASK_N <= HEAD_PAD


def _cmp_query_mask():
    q = np.arange(Q_TILE)
    thr = (q - (CMP_BLOCK - 1)) // CMP_STRIDE
    m = np.zeros((Q_TILE, HEAD_PAD), np.float32)
    m[q, _CMP_MASK_LANE + thr - _CMP_THR_MIN] = 1.0
    return jnp.asarray(np.tile(m, (HEADS_PER_GROUP, 1)), dtype=BF16)


def _cmp_kernel(q_ref, qm_ref, lc_ref, kc_ref, vc_ref, ovt_ref, gt_ref, o_ref, sel_ref,
                s_scr, p_scr, psum_scr, selt_scr, *, n_sel):
    tq = Q_TILE
    t = pl.program_id(2)
    t0 = t * tq
    nc = kc_ref.shape[1]
    ns = ovt_ref.shape[0]
    ii = lax.broadcasted_iota(jnp.int32, (nc, LANES), 0) - CMP_PER_TILE * t
    ll = lax.broadcasted_iota(jnp.int32, (nc, LANES), 1)
    hot = ii == ((ll >> 3) - ((ll & 7) + 1))
    late = ((ll >= _CMP_MASK_LANE) & (ll < _CMP_MASK_LANE + _CMP_MASK_N)
            & (ii > ll - (_CMP_MASK_LANE - _CMP_THR_MIN)))
    keys = jnp.where(late, NEG_INF, kc_ref[0].astype(F32)).astype(BF16)
    rhs = jnp.concatenate([keys, jnp.where(hot, 1.0, 0.0).astype(BF16)], axis=1)
    q = q_ref[...].reshape(HEADS_PER_GROUP * tq, HEAD_PAD) + qm_ref[...]
    lhs = jnp.concatenate([q, lc_ref[0]], axis=1)
    gt = gt_ref[0]
    n_chunk = max(nc // KEY_CHUNK, 1)
    wchunk = nc // n_chunk
    live = jnp.minimum((t0 + tq - CMP_BLOCK) // (CMP_STRIDE * wchunk) + 1, n_chunk)

    def attend(w):
        s_scr[:, :w] = _dot_nt(lhs, rhs[:w])
        for hh in range(HEADS_PER_GROUP):
            rows = slice(hh * tq, (hh + 1) * tq)
            s = s_scr[rows, :w]
            m = jnp.max(s, axis=-1, keepdims=True)
            e = jnp.exp2(s - m)
            l = jnp.sum(e, axis=-1, keepdims=True)
            p = e * jnp.where(m > 0.5 * NEG_INF, 1.0 / l, 0.0)
            if hh == 0:
                psum_scr[:, :w] = p
            else:
                psum_scr[:, :w] += p
            p_scr[rows, :w] = p.astype(BF16)
        o = _dot(p_scr[:, :w], vc_ref[0, :w, :])
        for hh in range(HEADS_PER_GROUP):
            o_ref[:, hh * D_V:(hh + 1) * D_V] = (o[hh * tq:(hh + 1) * tq, :D_V] * gt[:, hh:hh + 1]).astype(o_ref.dtype)
        nr = min(ns, w * CMP_STRIDE // SLC_BLOCK)
        psum = psum_scr[:, :w]
        p_hi = psum.astype(BF16)
        p_lo = (psum - p_hi.astype(F32)).astype(BF16)
        ovt = ovt_ref[:nr, :w]
        imp = _dot_nt(ovt, p_hi) + _dot_nt(ovt, p_lo)
        jr = lax.broadcasted_iota(jnp.int32, (nr, tq), 0)
        cur = (t0 + lax.broadcasted_iota(jnp.int32, (nr, tq), 1)) // SLC_BLOCK
        forced = (jr == 0) | (jr == cur) | (jr == cur - 1)
        n_forced = 3
        distinct = (t0 // SLC_BLOCK >= 2) & (n_sel > n_forced)
        imp = jnp.where(forced, jnp.where(distinct, -jnp.inf, FORCE_SCORE), imp)
        imp = jnp.where(jr > cur, -1.0, imp)
        jrf = jr.astype(F32)

        def pick(_, imp):
            m = jnp.max(imp, axis=0, keepdims=True)
            first = jnp.min(jnp.where(imp == m, jrf, 1e9), axis=0, keepdims=True)
            return jnp.where(jrf == first, -jnp.inf, imp)

        imp = lax.fori_loop(0, n_sel - jnp.where(distinct, n_forced, 0), pick, imp)
        selt_scr[0:nr] = jnp.where(imp == -jnp.inf, 1.0, 0.0)
        if nr < ns:
            selt_scr[nr:] = jnp.zeros((ns - nr, tq), F32)

    for k in range(1, n_chunk + 1):
        pl.when(live == k)(functools.partial(attend, k * wchunk))

    sel = selt_scr[...].T.astype(BF16)
    if sel_ref.shape[2] == ns:
        sel_ref[0] = sel
    else:
        sel_ref[0] = jnp.zeros(sel_ref.shape[1:], BF16)
        sel_ref[0, :, 0:ns] = sel


def _cmp_attention(qkv, lc, kcmp, vcmp, ovt, gates, batch):
    t = qkv.shape[1]
    s = t // batch
    nc = kcmp.shape[1] // batch
    ns = ovt.shape[0]
    nt = s // Q_TILE
    g, hg = N_KV_GROUPS, HEADS_PER_GROUP
    ns_pad = -(-ns // LANES) * LANES
    return pl.pallas_call(
        functools.partial(_cmp_kernel, n_sel=min(N_SLC, ns)),
        grid=(batch, g, nt),
        in_specs=[
            pl.BlockSpec((hg, Q_TILE, HEAD_PAD), lambda b, i, j: (i, b * nt + j, 0)),
            pl.BlockSpec((hg * Q_TILE, HEAD_PAD), lambda b, i, j: (0, 0)),
            pl.BlockSpec((1, hg * Q_TILE, BIAS_K), lambda b, i, j: (i, 0, 0)),
            pl.BlockSpec((1, nc, HEAD_PAD), lambda b, i, j: (i, b, 0)),
            pl.BlockSpec((1, nc, HEAD_PAD), lambda b, i, j: (i, b, 0)),
            pl.BlockSpec((ns, nc), lambda b, i, j: (0, 0)),
            pl.BlockSpec((1, Q_TILE, LANES), lambda b, i, j: (i, b * nt + j, 0)),
        ],
        out_specs=[
            pl.BlockSpec((Q_TILE, hg * D_V), lambda b, i, j: (b * nt + j, i)),
            pl.BlockSpec((1, Q_TILE, ns_pad), lambda b, i, j: (i, b * nt + j, 0)),
        ],
        out_shape=[
            jax.ShapeDtypeStruct((t, N_HEADS * D_V), BF16),
            jax.ShapeDtypeStruct((g, t, ns_pad), BF16),
        ],
        scratch_shapes=[pltpu.VMEM((hg * Q_TILE, nc), F32), pltpu.VMEM((hg * Q_TILE, nc), BF16),
                        pltpu.VMEM((Q_TILE, nc), F32), pltpu.VMEM((ns, Q_TILE), F32)],
        compiler_params=_params(("parallel", "parallel", "arbitrary"), 48),
        name="cmp_attention",
    )(qkv, _cmp_query_mask(), lc, kcmp, vcmp, ovt, gates)


_FAR_FACTOR = 4


def _sel_kernel(q_ref, ks_ref, vs_ref, sel_ref, tb_ref, gt_ref, o_ref,
                lhs_scr, s_scr, smax_scr, m_scr, acc_scr):
    tq = Q_TILE
    t = pl.program_id(2)
    n_var = sel_ref.shape[2] // LANES
    chunks_per_var = LANES * SLC_BLOCK // KEY_CHUNK
    blocks_per_chunk = KEY_CHUNK // SLC_BLOCK
    unsel = (1.0 - sel_ref[0].astype(F32)).astype(BF16)
    for var in range(n_var):
        @pl.when(t >= var * chunks_per_var)
        def _():
            for hh in range(HEADS_PER_GROUP):
                rows = slice(hh * tq, (hh + 1) * tq)
                lhs_scr[var, rows, 0:HEAD_PAD] = q_ref[hh]
                lhs_scr[var, rows, HEAD_PAD:] = unsel[:, var * LANES:(var + 1) * LANES]
    m_scr[...] = jnp.full(m_scr.shape, NEG_INF, F32)
    acc_scr[...] = jnp.zeros(acc_scr.shape, F32)
    lane = lax.broadcasted_iota(jnp.int32, (KEY_CHUNK, LANES), 1)
    kblk = lax.broadcasted_iota(jnp.int32, (KEY_CHUNK, LANES), 0) // SLC_BLOCK

    n_far = jnp.maximum(t - 1, 0)
    n_wide = n_far // _FAR_FACTOR
    wide = _FAR_FACTOR * KEY_CHUNK
    lane_w = lax.broadcasted_iota(jnp.int32, (wide, LANES), 1)
    kblk_w = lax.broadcasted_iota(jnp.int32, (wide, LANES), 0) // SLC_BLOCK

    def qk(start, nkeys, first_block, phantom):
        k = ks_ref[0, pl.ds(start, nkeys), :]
        flag = first_block % LANES + (kblk if nkeys == KEY_CHUNK else kblk_w)
        armed = (lane if nkeys == KEY_CHUNK else lane_w) == flag
        if phantom is not None:
            armed = armed | phantom
        rhs = jnp.concatenate([k, jnp.where(armed, NEG_INF, 0.0).astype(BF16)], axis=1)
        return _dot_nt(lhs_scr[first_block // LANES], rhs)

    def accumulate(start, nkeys, s, smax):
        v = vs_ref[0, pl.ds(start, nkeys), :]
        m_old = m_scr[...]
        m_new = jnp.maximum(m_old, smax)
        alpha = jnp.exp2(m_old - m_new)
        p = jnp.exp2(s - jnp.concatenate([m_new] * (nkeys // LANES), axis=1))
        acc_scr[...] = alpha * acc_scr[...] + _dot(p.astype(BF16), v)
        m_scr[...] = m_new

    def wide_start(i):
        return pl.multiple_of(jnp.minimum(i, t // _FAR_FACTOR) * wide, wide)

    def scores(i, buf):
        ii = jnp.minimum(i, t // _FAR_FACTOR)
        s = qk(wide_start(i), wide, ii * (wide // SLC_BLOCK), i >= n_wide)
        s_scr[buf] = s
        smax_scr[buf] = jnp.broadcast_to(jnp.max(s, axis=-1, keepdims=True), smax_scr.shape[1:])

    def update(i, buf):
        accumulate(wide_start(i), wide, s_scr[buf], smax_scr[buf])

    def narrow(c, section):
        start = pl.multiple_of(c * KEY_CHUNK, KEY_CHUNK)
        s = qk(start, KEY_CHUNK, c * blocks_per_chunk, None)
        if section is not None:
            s = s + tb_ref[0, section]
        accumulate(start, KEY_CHUNK, s, jnp.max(s, axis=-1, keepdims=True))

    scores(0, 0)

    def trip(_, base):
        for u in range(2):
            scores(base + u + 1, (u + 1) % 2)
            update(base + u, u % 2)
        return base + 2

    lax.fori_loop(0, n_wide // 2, trip, 0)

    @pl.when(n_wide % 2 == 1)
    def _():
        update(n_wide - 1, 0)

    for r in range(_FAR_FACTOR - 1):
        @pl.when(n_far % _FAR_FACTOR > r)
        def _():
            narrow(n_wide * _FAR_FACTOR + r, None)

    @pl.when(t >= 1)
    def _():
        narrow(t - 1, _TB_NEAR)
        narrow(t, _TB_DIAG)

    @pl.when(t == 0)
    def _():
        narrow(t, _TB_DIAG)
    gt = gt_ref[0]
    for hh in range(HEADS_PER_GROUP):
        rows = slice(hh * tq, (hh + 1) * tq)
        acc = acc_scr[rows]
        o = acc[:, :D_V] / acc[:, D_V:D_V + 1]
        o_ref[:, hh * D_V:(hh + 1) * D_V] = (o * gt[:, HEADS_PER_GROUP + hh:HEADS_PER_GROUP + hh + 1]).astype(o_ref.dtype)


def _sel_attention(qkv, sel, tb, gates, batch):
    t = qkv.shape[1]
    s = t // batch
    ns = sel.shape[2]
    nt = s // Q_TILE
    g, hg = N_KV_GROUPS, HEADS_PER_GROUP
    assert ns % LANES == 0 and HEAD_PAD == LANES
    assert s % (_FAR_FACTOR * KEY_CHUNK) == 0
    return pl.pallas_call(
        _sel_kernel,
        grid=(batch, g, nt),
        in_specs=[
            pl.BlockSpec((hg, Q_TILE, HEAD_PAD), lambda b, i, j: (i, b * nt + j, 0)),
            pl.BlockSpec((1, s, HEAD_PAD), lambda b, i, j: (SLOT_KS + i, b, 0)),
            pl.BlockSpec((1, s, HEAD_PAD), lambda b, i, j: (SLOT_VS + i, b, 0)),
            pl.BlockSpec((1, Q_TILE, ns), lambda b, i, j: (i, b * nt + j, 0)),
            pl.BlockSpec((1, _TB_SECTIONS, hg * Q_TILE, KEY_CHUNK), lambda b, i, j: (i, 0, 0, 0)),
            pl.BlockSpec((1, Q_TILE, LANES), lambda b, i, j: (i, b * nt + j, 0)),
        ],
        out_specs=pl.BlockSpec((Q_TILE, hg * D_V), lambda b, i, j: (b * nt + j, i)),
        out_shape=jax.ShapeDtypeStruct((t, N_HEADS * D_V), BF16),
        scratch_shapes=[pltpu.VMEM((ns // LANES, hg * Q_TILE, HEAD_PAD + LANES), BF16),
                        pltpu.VMEM((2, hg * Q_TILE, _FAR_FACTOR * KEY_CHUNK), F32),
                        pltpu.VMEM((2, hg * Q_TILE, LANES), F32),
                        pltpu.VMEM((hg * Q_TILE, LANES), F32),
                        pltpu.VMEM((hg * Q_TILE, HEAD_PAD), F32)],
        compiler_params=_params(("parallel", "parallel", "arbitrary"), 60),
        name="sel_attention",
    )(qkv, qkv, qkv, sel, tb, gates)


def _win_kernel(q_ref, k0_ref, k1_ref, k2_ref, v0_ref, v1_ref, v2_ref, tb_ref, mw_ref, gt_ref, o_ref):
    tq = Q_TILE
    t = pl.program_id(2)
    pen0 = jnp.where(t >= 2, 0.0, NEG_INF).astype(F32)
    pen1 = jnp.where(t >= 1, 0.0, NEG_INF).astype(F32)
    q = q_ref[...].reshape(HEADS_PER_GROUP * tq, HEAD_PAD)
    s0 = _dot_nt(q, k0_ref[0]) + (mw_ref[...] + pen0)
    s1 = _dot_nt(q, k1_ref[0]) + (tb_ref[0, _TB_NEAR] + pen1)
    s2 = _dot_nt(q, k2_ref[0]) + tb_ref[0, _TB_DIAG]
    m = jnp.maximum(jnp.maximum(jnp.max(s0, axis=-1, keepdims=True),
                                jnp.max(s1, axis=-1, keepdims=True)),
                    jnp.max(s2, axis=-1, keepdims=True))
    acc = (_dot(jnp.exp2(s0 - m).astype(BF16), v0_ref[0]) + _dot(jnp.exp2(s1 - m).astype(BF16), v1_ref[0])
           + _dot(jnp.exp2(s2 - m).astype(BF16), v2_ref[0]))
    o = acc[:, :D_V] / acc[:, D_V:D_V + 1]
    gt = gt_ref[0]
    for hh in range(HEADS_PER_GROUP):
        col = 2 * HEADS_PER_GROUP + hh
        o_ref[:, hh * D_V:(hh + 1) * D_V] = (o[hh * tq:(hh + 1) * tq] * gt[:, col:col + 1]).astype(o_ref.dtype)


def _win_attention(qkv, tb, gates, batch):
    t = qkv.shape[1]
    s = t // batch
    nt = s // Q_TILE
    g, hg = N_KV_GROUPS, HEADS_PER_GROUP
    assert WINDOW == 2 * KEY_CHUNK and Q_TILE == KEY_CHUNK
    q = np.arange(Q_TILE)[:, None]
    r = np.arange(KEY_CHUNK)[None, :]
    mw = jnp.asarray(np.tile(np.where(r > q, 0.0, NEG_INF).astype(np.float32), (hg, 1)))

    def kv_spec(slot, back):
        return pl.BlockSpec((1, KEY_CHUNK, HEAD_PAD),
                            lambda b, i, j: (slot + i, b * nt + jnp.maximum(j - back, 0), 0))

    return pl.pallas_call(
        _win_kernel,
        grid=(batch, g, nt),
        in_specs=[
            pl.BlockSpec((hg, Q_TILE, HEAD_PAD), lambda b, i, j: (i, b * nt + j, 0)),
            kv_spec(SLOT_KW, 2), kv_spec(SLOT_KW, 1), kv_spec(SLOT_KW, 0),
            kv_spec(SLOT_VW, 2), kv_spec(SLOT_VW, 1), kv_spec(SLOT_VW, 0),
            pl.BlockSpec((1, 2, hg * Q_TILE, KEY_CHUNK), lambda b, i, j: (i, 0, 0, 0)),
            pl.BlockSpec((hg * Q_TILE, KEY_CHUNK), lambda b, i, j: (0, 0)),
            pl.BlockSpec((1, Q_TILE, LANES), lambda b, i, j: (i, b * nt + j, 0)),
        ],
        out_specs=pl.BlockSpec((Q_TILE, hg * D_V), lambda b, i, j: (b * nt + j, i)),
        out_shape=jax.ShapeDtypeStruct((t, N_HEADS * D_V), BF16),
        compiler_params=_params(("parallel", "parallel", "arbitrary"), 48),
        name="win_attention",
    )(qkv, qkv, qkv, qkv, qkv, qkv, qkv, tb, mw, gates)


def _out_proj_kernel(a_ref, b_ref, c_ref, w_ref, x_ref, o_ref, s_scr):
    @pl.when(pl.program_id(1) == 0)
    def _():
        s_scr[...] = (a_ref[...].astype(F32) + b_ref[...].astype(F32) + c_ref[...].astype(F32)).astype(BF16)

    o_ref[...] = x_ref[...] + _dot(s_scr[...], w_ref[...])


def _out_proj(a, b, c, w, x, *, tm=512, tn=1024):
    t, d = a.shape
    n = w.shape[1]
    row = pl.BlockSpec((tm, d), lambda i, j: (i, 0))
    return pl.pallas_call(
        _out_proj_kernel,
        grid=(t // tm, n // tn),
        in_specs=[row, row, row,
                  pl.BlockSpec((d, tn), lambda i, j: (0, j)),
                  pl.BlockSpec((tm, tn), lambda i, j: (i, j))],
        out_specs=pl.BlockSpec((tm, tn), lambda i, j: (i, j)),
        out_shape=jax.ShapeDtypeStruct((t, n), F32),
        scratch_shapes=[pltpu.VMEM((tm, d), BF16)],
        compiler_params=_params(("parallel", "arbitrary"), 48),
        name="attn_out_proj",
    )(a, b, c, w, x)


def _ffn_kernel(x_ref, g_ref, wg_ref, wu_ref, wd_ref, gf_ref, o_ref, h_scr, acc_scr, *, nf, final_norm):
    f = pl.program_id(1)

    @pl.when(f == 0)
    def _():
        h_scr[...] = _rms(x_ref[...], g_ref[...]).astype(BF16)
        acc_scr[...] = jnp.zeros_like(acc_scr)

    h = h_scr[...]
    gate = _dot(h, wg_ref[...])
    up = _dot(h, wu_ref[...])
    act = (jax.nn.silu(gate) * up).astype(BF16)
    acc_scr[...] += _dot(act, wd_ref[...])

    @pl.when(f == nf - 1)
    def _():
        y = x_ref[...] + acc_scr[...]
        if final_norm:
            y = _rms(y, gf_ref[...])
        o_ref[...] = y


def _ffn(x, gamma, w_gu, w_down, gamma_final, *, final_norm, tm=512, tf=512):
    t, d = x.shape
    nf = D_FF // tf
    return pl.pallas_call(
        functools.partial(_ffn_kernel, nf=nf, final_norm=final_norm),
        grid=(t // tm, nf),
        in_specs=[
            pl.BlockSpec((tm, d), lambda i, f: (i, 0)),
            pl.BlockSpec((1, d), lambda i, f: (0, 0)),
            pl.BlockSpec((d, tf), lambda i, f: (0, f)),
            pl.BlockSpec((d, tf), lambda i, f: (0, nf + f)),
            pl.BlockSpec((tf, d), lambda i, f: (f, 0)),
            pl.BlockSpec((1, d), lambda i, f: (0, 0)),
        ],
        out_specs=pl.BlockSpec((tm, d), lambda i, f: (i, 0)),
        out_shape=jax.ShapeDtypeStruct((t, d), F32),
        scratch_shapes=[pltpu.VMEM((tm, d), BF16), pltpu.VMEM((tm, d), F32)],
        compiler_params=_params(("parallel", "arbitrary"), 56),
        name="swiglu_ffn",
    )(x, gamma, w_gu, w_gu, w_down, gamma_final)


_POOL_HIST = 16


def _pool_kernel(u_ref, up_ref, x_ref, wg_ref, sc_ref, wo_ref, o_ref, ext_scr, mix_scr, *, tm, seq):
    i = pl.program_id(0)

    @pl.when(pl.program_id(1) == 0)
    def _():
        t0 = (i * tm) % seq
        hist = up_ref[...]
        ext_scr[0:_POOL_HIST, :] = jnp.where(t0 == 0, jnp.zeros_like(hist), hist)
        ext_scr[_POOL_HIST:, :] = u_ref[...]
        tpos = t0 + lax.broadcasted_iota(jnp.int32, (tm, 1), 0)
        for gi, w in enumerate(POOL_WINDOWS):
            cols = slice(gi * POOL_GROUP, (gi + 1) * POOL_GROUP)
            tot = ext_scr[_POOL_HIST:, cols]
            for back in range(1, w):
                tot = tot + ext_scr[_POOL_HIST - back:_POOL_HIST - back + tm, cols]
            cnt = jnp.minimum(tpos + 1, w).astype(F32)
            pooled = tot / cnt - ext_scr[_POOL_HIST:, cols]
            mixed = _dot(pooled.astype(BF16), wg_ref[gi]) * sc_ref[:, cols]
            mix_scr[:, cols] = mixed.astype(BF16)

    o_ref[...] = x_ref[...] + _dot(mix_scr[...], wo_ref[...])


def _pool_mix(u, x, w_grp, scale, w_out, seq, *, tm=512, tn=1024):
    t, d = u.shape
    hb = tm // _POOL_HIST
    return pl.pallas_call(
        functools.partial(_pool_kernel, tm=tm, seq=seq),
        grid=(t // tm, d // tn),
        in_specs=[
            pl.BlockSpec((tm, d), lambda i, j: (i, 0)),
            pl.BlockSpec((_POOL_HIST, d), lambda i, j: (jnp.maximum(i * hb - 1, 0), 0)),
            pl.BlockSpec((tm, tn), lambda i, j: (i, j)),
            pl.BlockSpec((N_POOL_GROUPS, POOL_GROUP, POOL_GROUP), lambda i, j: (0, 0, 0)),
            pl.BlockSpec((1, d), lambda i, j: (0, 0)),
            pl.BlockSpec((d, tn), lambda i, j: (0, j)),
        ],
        out_specs=pl.BlockSpec((tm, tn), lambda i, j: (i, j)),
        out_shape=jax.ShapeDtypeStruct((t, d), F32),
        scratch_shapes=[pltpu.VMEM((tm + _POOL_HIST, d), F32), pltpu.VMEM((tm, d), BF16)],
        compiler_params=_params(("parallel", "arbitrary"), 56),
        name="pool_mixer",
    )(u, u, x, w_grp, scale, w_out)


def _pad_heads(w, n, dh):
    d = w.shape[0]
    return jnp.pad(w.reshape(d, n, dh), ((0, 0), (0, 0), (0, HEAD_PAD - dh))).reshape(d, n * HEAD_PAD)


def _nsa_weights(w_in, rel_bias):
    g, hg = N_KV_GROUPS, HEADS_PER_GROUP
    qw, kw, vw = N_HEADS * D_K, g * D_K, g * D_V
    offs = np.cumsum([0, qw, kw, vw, kw, vw, kw, vw])
    wq, wkc, wvc, wks, wvs, wkw, wvw = [w_in[:, offs[i]:offs[i + 1]] for i in range(7)]
    wgt = w_in[:, offs[7]:]
    w_heads = jnp.concatenate([
        _pad_heads(wq, N_HEADS, D_K), _pad_heads(wks, g, D_K), _pad_heads(wkw, g, D_K),
        _pad_heads(wvs, g, D_V), _pad_heads(wvw, g, D_V)], axis=1).astype(BF16)
    scale = np.ones((N_SLOTS, HEAD_PAD), np.float32)
    scale[:N_HEADS] = D_K ** -0.5 * LOG2E
    add = np.zeros((N_SLOTS, HEAD_PAD), np.float32)
    add[SLOT_KS:SLOT_VS, D_K] = 1.0
    add[SLOT_VS:, D_V] = 1.0
    add = jnp.asarray(add).at[:N_HEADS, D_K].set(rel_bias[N_BUCKETS - 1].astype(F32) * LOG2E)
    wg = wgt.reshape(-1, 3, g, hg).transpose(0, 2, 1, 3).reshape(-1, g, 3 * hg)
    wg = jnp.pad(wg, ((0, 0), (0, 0), (0, LANES - 3 * hg))).reshape(-1, g * LANES)
    w_side = jnp.concatenate([wkc, wvc, wg], axis=1).astype(BF16)
    return w_heads, jnp.asarray(scale).reshape(1, -1), add.reshape(1, -1), w_side


def _overlap_matrix_t(nc, ns):
    i = np.arange(nc)[None, :]
    j = np.arange(ns)[:, None]
    ratio = SLC_BLOCK // CMP_STRIDE
    diff = i - ratio * j
    w = np.zeros((ns, nc), np.float32)
    for n in range(CMP_BLOCK // CMP_STRIDE):
        w += ((diff + n >= 0) & (diff + n < ratio)).astype(np.float32)
    w[:, nc - 1] = 0.0
    return jnp.asarray(w, dtype=BF16)


def _nsa_layer(x, gamma, rel_bias, w_in, pos_k, pos_v, ck_w1, ck_b1, ck_w2, cv_w1, cv_b1, cv_w2,
               w_out, batch):
    t = x.shape[0]
    seq = t // batch
    assert seq % Q_TILE == 0
    w_heads, sc_row, ad_row, w_side = _nsa_weights(w_in, rel_bias)
    tb, lc = _build_tables(rel_bias)
    qkv = _norm_heads(x, gamma, w_heads, sc_row, ad_row)
    kc, vc, gates = _norm_side(x, gamma, w_side)
    kcmp = _compress(kc, pos_k, ck_w1, ck_b1, ck_w2, D_K, D_K, batch)
    vcmp = _compress(vc, pos_v, cv_w1, cv_b1, cv_w2, D_V, None, batch)
    ovt = _overlap_matrix_t(seq // CMP_STRIDE, seq // SLC_BLOCK)
    o_c, sel = _cmp_attention(qkv, lc, kcmp, vcmp, ovt, gates, batch)
    o_s = _sel_attention(qkv, sel, tb, gates, batch)
    o_w = _win_attention(qkv, tb, gates, batch)
    return _out_proj(o_c, o_s, o_w, w_out.astype(BF16), x)


def kernel(x, norm_mix, norm_ffn, norm_final, rel_bias, nsa_w_in, nsa_pos_k, nsa_pos_v, nsa_ck_w1, nsa_ck_b1, nsa_ck_w2, nsa_cv_w1, nsa_cv_b1, nsa_cv_w2, nsa_w_out, pool_w_in, pool_w_grp, pool_scale, pool_w_out, ffn_w_gu, ffn_w_down):
    batch, seq, d = x.shape
    h = x.reshape(batch * seq, d)
    gfin = norm_final.reshape(1, d)
    h = _nsa_layer(h, norm_mix[0].reshape(1, d), rel_bias, nsa_w_in[0], nsa_pos_k[0], nsa_pos_v[0],
                   nsa_ck_w1[0], nsa_ck_b1[0], nsa_ck_w2[0], nsa_cv_w1[0], nsa_cv_b1[0], nsa_cv_w2[0],
                   nsa_w_out[0], batch)
    h = _ffn(h, norm_ffn[0].reshape(1, d), ffn_w_gu[0].astype(BF16), ffn_w_down[0].astype(BF16), gfin,
             final_norm=False)
    u = _norm_mm(h, norm_mix[1].reshape(1, d), pool_w_in[0].astype(BF16))
    h = _pool_mix(u, h, pool_w_grp[0].astype(BF16), pool_scale[0].reshape(1, d),
                  pool_w_out[0].astype(BF16), seq)
    h = _ffn(h, norm_ffn[1].reshape(1, d), ffn_w_gu[1].astype(BF16), ffn_w_down[1].astype(BF16), gfin,
             final_norm=True)
    return h.reshape(batch, seq, d)
```

```python
import functools
import math

import numpy as np
import jax
import jax.numpy as jnp
from jax import lax
from jax.experimental import pallas as pl
from jax.experimental.pallas import tpu as pltpu

D_MODEL = 2048
N_HEADS = 32
N_KV_GROUPS = 4
HEADS_PER_GROUP = N_HEADS // N_KV_GROUPS
D_K = 96
D_V = 64
CMP_BLOCK = 32
CMP_STRIDE = 16
CMP_HIDDEN = 256
SLC_BLOCK = 64
N_SLC = 16
WINDOW = 512
N_BUCKETS = 32
MAX_DISTANCE = 128
POOL_WINDOWS = (2, 4, 8, 16)
N_POOL_GROUPS = len(POOL_WINDOWS)
POOL_GROUP = D_MODEL // N_POOL_GROUPS
D_FF = -(-8 * D_MODEL // (3 * 256)) * 256
EPS = 1e-6
NEG_INF = -1e30
FORCE_SCORE = 1e9
LOG2E = math.log2(math.e)

LANES = 128
HEAD_PAD = LANES
Q_TILE = 256
KEY_CHUNK = 256
CMP_PER_TILE = Q_TILE // CMP_STRIDE
BIAS_K = CMP_PER_TILE * 8
SLOT_Q = 0
SLOT_KS = N_HEADS
SLOT_KW = SLOT_KS + N_KV_GROUPS
SLOT_VS = SLOT_KW + N_KV_GROUPS
SLOT_VW = SLOT_VS + N_KV_GROUPS
N_SLOTS = SLOT_VW + N_KV_GROUPS

F32 = jnp.float32
BF16 = jnp.bfloat16


def _bucket_np(dist):
    n = np.maximum(dist, 0)
    max_exact = N_BUCKETS // 2
    nf = np.maximum(n, 1).astype(np.float32)
    scaled = (np.log(nf / np.float32(max_exact)) / np.float32(math.log(MAX_DISTANCE / max_exact))
              * np.float32(N_BUCKETS - max_exact)).astype(np.float32)
    large = np.minimum(max_exact + scaled.astype(np.int32), N_BUCKETS - 1)
    return np.where(n < max_exact, n, large).astype(np.int32)


_FAR_DIST = int(np.max(np.nonzero(_bucket_np(np.arange(4096)) != N_BUCKETS - 1)[0])) + 1
assert _FAR_DIST <= CMP_STRIDE * 8 + (CMP_STRIDE - 1) - (CMP_BLOCK - 1) + 1 and _FAR_DIST <= KEY_CHUNK
_MASKED_BUCKET = N_BUCKETS
_ZERO_BUCKET = N_BUCKETS + 1


def _toeplitz_bucket_map():
    q = np.arange(Q_TILE)[:, None]
    r = np.arange(2 * KEY_CHUNK)[None, :]
    d = q + KEY_CHUNK - r
    b = _bucket_np(d)
    b = np.where(d >= _FAR_DIST, _ZERO_BUCKET, b)
    return np.where(d < 0, _MASKED_BUCKET, b).astype(np.int32)


def _cmp_bucket_map():
    q = np.arange(Q_TILE)[:, None]
    k = np.arange(BIAS_K)[None, :]
    a, b = q // CMP_STRIDE, q % CMP_STRIDE
    a2, m = k // 8, k % 8 + 1
    d = CMP_STRIDE * m + b - (CMP_BLOCK - 1)
    bk = _bucket_np(d)
    bk = np.where((d >= _FAR_DIST) | (d < 0) | (a != a2), _ZERO_BUCKET, bk)
    return bk.astype(np.int32)


def _params(sem, vmem_mb):
    return pltpu.CompilerParams(dimension_semantics=sem, vmem_limit_bytes=vmem_mb * 1024 * 1024)


def _rms(x, g):
    ms = jnp.mean(x * x, axis=-1, keepdims=True)
    return x * lax.rsqrt(ms + EPS) * g


def _dot(a, b):
    return jnp.dot(a, b, preferred_element_type=F32)


def _dot_nt(a, b):
    return lax.dot_general(a, b, (((1,), (1,)), ((), ())), preferred_element_type=F32)


def _tables_kernel(tbl_ref, bm_ref, bc_ref, tb_ref, lc_ref):
    h = pl.program_id(0) * HEADS_PER_GROUP + pl.program_id(1)
    far = tbl_ref[N_BUCKETS - 1, h]
    bm = bm_ref[...]
    bc = bc_ref[...]
    tb = jnp.where(bm == _MASKED_BUCKET, NEG_INF, 0.0).astype(F32)
    lc = jnp.zeros(bc.shape, F32)
    for b in range(N_BUCKETS - 1):
        val = (tbl_ref[b, h] - far) * LOG2E
        tb = jnp.where(bm == b, val, tb)
        lc = jnp.where(bc == b, val, lc)
    tb_ref[0, _TB_NEAR] = tb[:, :KEY_CHUNK]
    tb_ref[0, _TB_DIAG] = tb[:, KEY_CHUNK:]
    lc_ref[0] = lc.astype(BF16)


_TB_NEAR, _TB_DIAG = 0, 1
_TB_SECTIONS = 2


def _build_tables(rel_bias):
    bm = jnp.asarray(_toeplitz_bucket_map())
    bc = jnp.asarray(_cmp_bucket_map())
    g, hg = N_KV_GROUPS, HEADS_PER_GROUP
    return pl.pallas_call(
        _tables_kernel,
        grid=(g, hg),
        in_specs=[
            pl.BlockSpec(memory_space=pltpu.SMEM),
            pl.BlockSpec(bm.shape, lambda i, j: (0, 0)),
            pl.BlockSpec(bc.shape, lambda i, j: (0, 0)),
        ],
        out_specs=[
            pl.BlockSpec((1, _TB_SECTIONS, Q_TILE, KEY_CHUNK), lambda i, j: (i, 0, j, 0)),
            pl.BlockSpec((1, Q_TILE, BIAS_K), lambda i, j: (i, j, 0)),
        ],
        out_shape=[
            jax.ShapeDtypeStruct((g, _TB_SECTIONS, hg * Q_TILE, KEY_CHUNK), F32),
            jax.ShapeDtypeStruct((g, hg * Q_TILE, BIAS_K), BF16),
        ],
        compiler_params=_params(("arbitrary", "arbitrary"), 32),
        name="bias_tables",
    )(rel_bias, bm, bc)


def _norm_heads_kernel(x_ref, g_ref, w_ref, sc_ref, ad_ref, o_ref, h_scr, *, slabs):
    @pl.when(pl.program_id(1) == 0)
    def _():
        h_scr[...] = _rms(x_ref[...], g_ref[...]).astype(BF16)

    acc = _dot(h_scr[...], w_ref[...]) * sc_ref[...] + ad_ref[...]
    for k in range(slabs):
        o_ref[k] = acc[:, k * HEAD_PAD:(k + 1) * HEAD_PAD].astype(o_ref.dtype)


def _norm_heads(x, gamma, w, scale_row, add_row, *, tm=512, slabs=8):
    t, d = x.shape
    n = w.shape[1]
    tn = slabs * HEAD_PAD
    return pl.pallas_call(
        functools.partial(_norm_heads_kernel, slabs=slabs),
        grid=(t // tm, n // tn),
        in_specs=[
            pl.BlockSpec((tm, d), lambda i, j: (i, 0)),
            pl.BlockSpec((1, d), lambda i, j: (0, 0)),
            pl.BlockSpec((d, tn), lambda i, j: (0, j)),
            pl.BlockSpec((1, tn), lambda i, j: (0, j)),
            pl.BlockSpec((1, tn), lambda i, j: (0, j)),
        ],
        out_specs=pl.BlockSpec((slabs, tm, HEAD_PAD), lambda i, j: (j, i, 0)),
        out_shape=jax.ShapeDtypeStruct((n // HEAD_PAD, t, HEAD_PAD), BF16),
        scratch_shapes=[pltpu.VMEM((tm, d), BF16)],
        compiler_params=_params(("parallel", "arbitrary"), 48),
        name="norm_proj_heads",
    )(x, gamma, w, scale_row, add_row)


def _norm_side_kernel(x_ref, g_ref, w_ref, kc_ref, vc_ref, gt_ref, *, kw, vw):
    h = _rms(x_ref[...], g_ref[...]).astype(BF16)
    acc = _dot(h, w_ref[...])
    kc_ref[...] = acc[:, :kw]
    vc_ref[...] = acc[:, kw:kw + vw]
    for g in range(N_KV_GROUPS):
        lo = kw + vw + g * LANES
        gt_ref[g] = jax.nn.sigmoid(acc[:, lo:lo + LANES])


def _norm_side(x, gamma, w, *, tm=512):
    t, d = x.shape
    kw, vw = N_KV_GROUPS * D_K, N_KV_GROUPS * D_V
    n = w.shape[1]
    return pl.pallas_call(
        functools.partial(_norm_side_kernel, kw=kw, vw=vw),
        grid=(t // tm,),
        in_specs=[
            pl.BlockSpec((tm, d), lambda i: (i, 0)),
            pl.BlockSpec((1, d), lambda i: (0, 0)),
            pl.BlockSpec((d, n), lambda i: (0, 0)),
        ],
        out_specs=[
            pl.BlockSpec((tm, kw), lambda i: (i, 0)),
            pl.BlockSpec((tm, vw), lambda i: (i, 0)),
            pl.BlockSpec((N_KV_GROUPS, tm, LANES), lambda i: (0, i, 0)),
        ],
        out_shape=[
            jax.ShapeDtypeStruct((t, kw), F32),
            jax.ShapeDtypeStruct((t, vw), F32),
            jax.ShapeDtypeStruct((N_KV_GROUPS, t, LANES), F32),
        ],
        compiler_params=_params(("parallel",), 48),
        name="norm_proj_side",
    )(x, gamma, w)


def _norm_mm_kernel(x_ref, g_ref, w_ref, o_ref, h_scr):
    @pl.when(pl.program_id(1) == 0)
    def _():
        h_scr[...] = _rms(x_ref[...], g_ref[...]).astype(BF16)

    o_ref[...] = _dot(h_scr[...], w_ref[...])


def _norm_mm(x, gamma, w, *, tm=512, tn=1024):
    t, d = x.shape
    n = w.shape[1]
    return pl.pallas_call(
        _norm_mm_kernel,
        grid=(t // tm, n // tn),
        in_specs=[
            pl.BlockSpec((tm, d), lambda i, j: (i, 0)),
            pl.BlockSpec((1, d), lambda i, j: (0, 0)),
            pl.BlockSpec((d, tn), lambda i, j: (0, j)),
        ],
        out_specs=pl.BlockSpec((tm, tn), lambda i, j: (i, j)),
        out_shape=jax.ShapeDtypeStruct((t, n), F32),
        scratch_shapes=[pltpu.VMEM((tm, d), BF16)],
        compiler_params=_params(("parallel", "arbitrary"), 48),
        name="norm_matmul",
    )(x, gamma, w)


def _compress_kernel(x_ref, pa_ref, pb_ref, wa_ref, wb_ref, b1_ref, w2_ref, ad_ref, o_ref,
                     acca, accb, *, nk):
    k = pl.program_id(1)

    @pl.when(k == 0)
    def _():
        acca[...] = jnp.zeros_like(acca)
        accb[...] = jnp.zeros_like(accb)

    x = x_ref[...]
    acca[...] += _dot((x + pa_ref[...]).astype(BF16), wa_ref[...])
    accb[...] += _dot((x + pb_ref[...]).astype(BF16), wb_ref[...])

    @pl.when(k == nk - 1)
    def _():
        rows = accb.shape[0]
        hid = acca[...] + pltpu.roll(accb[...], rows - 1, 0) + b1_ref[...]
        hid = jax.nn.gelu(hid)
        out = _dot(hid.astype(BF16), w2_ref[...]) + ad_ref[...]
        for g in range(N_KV_GROUPS):
            o_ref[g] = out[:, g * HEAD_PAD:(g + 1) * HEAD_PAD].astype(o_ref.dtype)


def _compress(kv, pos, w1, b1, w2, dh, one_col, batch):
    t = kv.shape[0]
    g = N_KV_GROUPS
    chunks = t // CMP_STRIDE
    rows = chunks // batch
    width = CMP_STRIDE * g * dh
    x = kv.reshape(chunks, width)
    eye = jnp.eye(g, dtype=F32)
    w1r = w1.reshape(2, CMP_STRIDE, dh, CMP_HIDDEN)

    def expand(wh):
        return jnp.einsum('rdj,gh->rgdhj', wh, eye).reshape(width, g * CMP_HIDDEN).astype(BF16)

    wa, wb = expand(w1r[0]), expand(w1r[1])
    posr = pos.reshape(2, CMP_STRIDE, 1, dh)
    pa = jnp.broadcast_to(posr[0], (CMP_STRIDE, g, dh)).reshape(1, width)
    pb = jnp.broadcast_to(posr[1], (CMP_STRIDE, g, dh)).reshape(1, width)
    b1r = jnp.tile(b1.reshape(1, CMP_HIDDEN), (1, g))
    w2p = jnp.pad(w2, ((0, 0), (0, HEAD_PAD - dh)))
    w2x = jnp.einsum('jd,gh->gjhd', w2p, eye).reshape(g * CMP_HIDDEN, g * HEAD_PAD).astype(BF16)
    ad = np.zeros((1, g * HEAD_PAD), np.float32)
    if one_col is not None:
        ad[0, one_col::HEAD_PAD] = 1.0
    ad = jnp.asarray(ad)
    kt = width // 8
    nk = width // kt
    return pl.pallas_call(
        functools.partial(_compress_kernel, nk=nk),
        grid=(batch, nk),
        in_specs=[
            pl.BlockSpec((rows, kt), lambda b, k: (b, k)),
            pl.BlockSpec((1, kt), lambda b, k: (0, k)),
            pl.BlockSpec((1, kt), lambda b, k: (0, k)),
            pl.BlockSpec((kt, g * CMP_HIDDEN), lambda b, k: (k, 0)),
            pl.BlockSpec((kt, g * CMP_HIDDEN), lambda b, k: (k, 0)),
            pl.BlockSpec((1, g * CMP_HIDDEN), lambda b, k: (0, 0)),
            pl.BlockSpec((g * CMP_HIDDEN, g * HEAD_PAD), lambda b, k: (0, 0)),
            pl.BlockSpec((1, g * HEAD_PAD), lambda b, k: (0, 0)),
        ],
        out_specs=pl.BlockSpec((g, rows, HEAD_PAD), lambda b, k: (0, b, 0)),
        out_shape=jax.ShapeDtypeStruct((g, chunks, HEAD_PAD), BF16),
        scratch_shapes=[pltpu.VMEM((rows, g * CMP_HIDDEN), F32),
                        pltpu.VMEM((rows, g * CMP_HIDDEN), F32)],
        compiler_params=_params(("parallel", "arbitrary"), 48),
        name="compress_mlp",
    )(x, pa, pb, wa, wb, b1r, w2x, ad)


_CMP_MASK_LANE = D_K + 1
_CMP_THR_MIN = -((CMP_BLOCK - 1 + CMP_STRIDE - 1) // CMP_STRIDE)
_CMP_MASK_N = (Q_TILE - CMP_BLOCK) // CMP_STRIDE - _CMP_THR_MIN + 1
assert _CMP_MASK_LANE + _CMP_MASK_N <= HEAD_PAD


def _cmp_query_mask():
    q = np.arange(Q_TILE)
    thr = (q - (CMP_BLOCK - 1)) // CMP_STRIDE
    m = np.zeros((Q_TILE, HEAD_PAD), np.float32)
    m[q, _CMP_MASK_LANE + thr - _CMP_THR_MIN] = 1.0
    return jnp.asarray(np.tile(m, (HEADS_PER_GROUP, 1)), dtype=BF16)


def _cmp_kernel(q_ref, qm_ref, lc_ref, kc_ref, vc_ref, ovt_ref, gt_ref, o_ref, sel_ref,
                s_scr, p_scr, psum_scr, selt_scr, *, n_sel):
    tq = Q_TILE
    t = pl.program_id(2)
    t0 = t * tq
    nc = kc_ref.shape[1]
    ns = ovt_ref.shape[0]
    ii = lax.broadcasted_iota(jnp.int32, (nc, LANES), 0) - CMP_PER_TILE * t
    ll = lax.broadcasted_iota(jnp.int32, (nc, LANES), 1)
    hot = ii == ((ll >> 3) - ((ll & 7) + 1))
    late = ((ll >= _CMP_MASK_LANE) & (ll < _CMP_MASK_LANE + _CMP_MASK_N)
            & (ii > ll - (_CMP_MASK_LANE - _CMP_THR_MIN)))
    keys = jnp.where(late, NEG_INF, kc_ref[0].astype(F32)).astype(BF16)
    rhs = jnp.concatenate([keys, jnp.where(hot, 1.0, 0.0).astype(BF16)], axis=1)
    q = q_ref[...].reshape(HEADS_PER_GROUP * tq, HEAD_PAD) + qm_ref[...]
    lhs = jnp.concatenate([q, lc_ref[0]], axis=1)
    gt = gt_ref[0]
    n_chunk = max(nc // KEY_CHUNK, 1)
    wchunk = nc // n_chunk
    live = jnp.minimum((t0 + tq - CMP_BLOCK) // (CMP_STRIDE * wchunk) + 1, n_chunk)

    def attend(w):
        s_scr[:, :w] = _dot_nt(lhs, rhs[:w])
        for hh in range(HEADS_PER_GROUP):
            rows = slice(hh * tq, (hh + 1) * tq)
            s = s_scr[rows, :w]
            m = jnp.max(s, axis=-1, keepdims=True)
            e = jnp.exp2(s - m)
            l = jnp.sum(e, axis=-1, keepdims=True)
            p = e * jnp.where(m > 0.5 * NEG_INF, 1.0 / l, 0.0)
            if hh == 0:
                psum_scr[:, :w] = p
            else:
                psum_scr[:, :w] += p
            p_scr[rows, :w] = p.astype(BF16)
        o = _dot(p_scr[:, :w], vc_ref[0, :w, :])
        for hh in range(HEADS_PER_GROUP):
            o_ref[:, hh * D_V:(hh + 1) * D_V] = (o[hh * tq:(hh + 1) * tq, :D_V] * gt[:, hh:hh + 1]).astype(o_ref.dtype)
        nr = min(ns, w * CMP_STRIDE // SLC_BLOCK)
        psum = psum_scr[:, :w]
        p_hi = psum.astype(BF16)
        p_lo = (psum - p_hi.astype(F32)).astype(BF16)
        ovt = ovt_ref[:nr, :w]
        imp = _dot_nt(ovt, p_hi) + _dot_nt(ovt, p_lo)
        jr = lax.broadcasted_iota(jnp.int32, (nr, tq), 0)
        cur = (t0 + lax.broadcasted_iota(jnp.int32, (nr, tq), 1)) // SLC_BLOCK
        forced = (jr == 0) | (jr == cur) | (jr == cur - 1)
        n_forced = 3
        distinct = (t0 // SLC_BLOCK >= 2) & (n_sel > n_forced)
        imp = jnp.where(forced, jnp.where(distinct, -jnp.inf, FORCE_SCORE), imp)
        imp = jnp.where(jr > cur, -1.0, imp)
        jrf = jr.astype(F32)

        def pick(_, imp):
            m = jnp.max(imp, axis=0, keepdims=True)
            first = jnp.min(jnp.where(imp == m, jrf, 1e9), axis=0, keepdims=True)
            return jnp.where(jrf == first, -jnp.inf, imp)

        imp = lax.fori_loop(0, n_sel - jnp.where(distinct, n_forced, 0), pick, imp)
        selt_scr[0:nr] = jnp.where(imp == -jnp.inf, 1.0, 0.0)
        if nr < ns:
            selt_scr[nr:] = jnp.zeros((ns - nr, tq), F32)

    for k in range(1, n_chunk + 1):
        pl.when(live == k)(functools.partial(attend, k * wchunk))

    sel = selt_scr[...].T.astype(BF16)
    if sel_ref.shape[2] == ns:
        sel_ref[0] = sel
    else:
        sel_ref[0] = jnp.zeros(sel_ref.shape[1:], BF16)
        sel_ref[0, :, 0:ns] = sel


def _cmp_attention(qkv, lc, kcmp, vcmp, ovt, gates, batch):
    t = qkv.shape[1]
    s = t // batch
    nc = kcmp.shape[1] // batch
    ns = ovt.shape[0]
    nt = s // Q_TILE
    g, hg = N_KV_GROUPS, HEADS_PER_GROUP
    ns_pad = -(-ns // LANES) * LANES
    return pl.pallas_call(
        functools.partial(_cmp_kernel, n_sel=min(N_SLC, ns)),
        grid=(batch, g, nt),
        in_specs=[
            pl.BlockSpec((hg, Q_TILE, HEAD_PAD), lambda b, i, j: (i, b * nt + j, 0)),
            pl.BlockSpec((hg * Q_TILE, HEAD_PAD), lambda b, i, j: (0, 0)),
            pl.BlockSpec((1, hg * Q_TILE, BIAS_K), lambda b, i, j: (i, 0, 0)),
            pl.BlockSpec((1, nc, HEAD_PAD), lambda b, i, j: (i, b, 0)),
            pl.BlockSpec((1, nc, HEAD_PAD), lambda b, i, j: (i, b, 0)),
            pl.BlockSpec((ns, nc), lambda b, i, j: (0, 0)),
            pl.BlockSpec((1, Q_TILE, LANES), lambda b, i, j: (i, b * nt + j, 0)),
        ],
        out_specs=[
            pl.BlockSpec((Q_TILE, hg * D_V), lambda b, i, j: (b * nt + j, i)),
            pl.BlockSpec((1, Q_TILE, ns_pad), lambda b, i, j: (i, b * nt + j, 0)),
        ],
        out_shape=[
            jax.ShapeDtypeStruct((t, N_HEADS * D_V), BF16),
            jax.ShapeDtypeStruct((g, t, ns_pad), BF16),
        ],
        scratch_shapes=[pltpu.VMEM((hg * Q_TILE, nc), F32), pltpu.VMEM((hg * Q_TILE, nc), BF16),
                        pltpu.VMEM((Q_TILE, nc), F32), pltpu.VMEM((ns, Q_TILE), F32)],
        compiler_params=_params(("parallel", "parallel", "arbitrary"), 48),
        name="cmp_attention",
    )(qkv, _cmp_query_mask(), lc, kcmp, vcmp, ovt, gates)


_FAR_FACTOR = 4


def _sel_kernel(q_ref, ks_ref, vs_ref, sel_ref, tb_ref, gt_ref, o_ref,
                lhs_scr, s_scr, smax_scr, m_scr, acc_scr):
    tq = Q_TILE
    t = pl.program_id(2)
    n_var = sel_ref.shape[2] // LANES
    chunks_per_var = LANES * SLC_BLOCK // KEY_CHUNK
    blocks_per_chunk = KEY_CHUNK // SLC_BLOCK
    unsel = (1.0 - sel_ref[0].astype(F32)).astype(BF16)
    def build_lhs(var):
        for hh in range(HEADS_PER_GROUP):
            rows = slice(hh * tq, (hh + 1) * tq)
            lhs_scr[var, rows, 0:HEAD_PAD] = q_ref[hh]
            lhs_scr[var, rows, HEAD_PAD:] = unsel[:, var * LANES:(var + 1) * LANES]

    build_lhs(0)
    for var in range(1, n_var):
        pl.when(t >= var * chunks_per_var)(functools.partial(build_lhs, var))
    m_scr[...] = jnp.full(m_scr.shape, NEG_INF, F32)
    acc_scr[...] = jnp.zeros(acc_scr.shape, F32)
    lane = lax.broadcasted_iota(jnp.int32, (KEY_CHUNK, LANES), 1)
    kblk = lax.broadcasted_iota(jnp.int32, (KEY_CHUNK, LANES), 0) // SLC_BLOCK

    n_far = jnp.maximum(t - 1, 0)
    n_wide = n_far // _FAR_FACTOR
    wide = _FAR_FACTOR * KEY_CHUNK
    lane_w = lax.broadcasted_iota(jnp.int32, (wide, LANES), 1)
    kblk_w = lax.broadcasted_iota(jnp.int32, (wide, LANES), 0) // SLC_BLOCK

    def qk(start, nkeys, first_block, phantom):
        k = ks_ref[0, pl.ds(start, nkeys), :]
        flag = first_block % LANES + (kblk if nkeys == KEY_CHUNK else kblk_w)
        armed = (lane if nkeys == KEY_CHUNK else lane_w) == flag
        if phantom is not None:
            armed = armed | phantom
        rhs = jnp.concatenate([k, jnp.where(armed, NEG_INF, 0.0).astype(BF16)], axis=1)
        return _dot_nt(lhs_scr[first_block // LANES], rhs)

    def accumulate(start, nkeys, s, smax):
        v = vs_ref[0, pl.ds(start, nkeys), :]
        m_old = m_scr[...]
        m_new = jnp.maximum(m_old, smax)
        alpha = jnp.exp2(m_old - m_new)
        p = jnp.exp2(s - jnp.concatenate([m_new] * (nkeys // LANES), axis=1))
        acc_scr[...] = alpha * acc_scr[...] + _dot(p.astype(BF16), v)
        m_scr[...] = m_new

    def wide_start(i):
        return pl.multiple_of(jnp.minimum(i, t // _FAR_FACTOR) * wide, wide)

    def scores(i, buf):
        ii = jnp.minimum(i, t // _FAR_FACTOR)
        s = qk(wide_start(i), wide, ii * (wide // SLC_BLOCK), i >= n_wide)
        s_scr[buf] = s
        smax_scr[buf] = jnp.broadcast_to(jnp.max(s, axis=-1, keepdims=True), smax_scr.shape[1:])

    def update(i, buf):
        accumulate(wide_start(i), wide, s_scr[buf], smax_scr[buf])

    def narrow(c, section):
        start = pl.multiple_of(c * KEY_CHUNK, KEY_CHUNK)
        s = qk(start, KEY_CHUNK, c * blocks_per_chunk, None)
        if section is not None:
            s = s + tb_ref[0, section]
        accumulate(start, KEY_CHUNK, s, jnp.max(s, axis=-1, keepdims=True))

    scores(0, 0)

    def trip(_, base):
        for u in range(2):
            scores(base + u + 1, (u + 1) % 2)
            update(base + u, u % 2)
        return base + 2

    lax.fori_loop(0, n_wide // 2, trip, 0)

    @pl.when(n_wide % 2 == 1)
    def _():
        update(n_wide - 1, 0)

    for left in range(_FAR_FACTOR):
        @pl.when((t >= 1) & (n_far % _FAR_FACTOR == left))
        def _():
            for u in range(left):
                narrow(n_wide * _FAR_FACTOR + u, None)
            narrow(t - 1, _TB_NEAR)
            narrow(t, _TB_DIAG)

    @pl.when(t == 0)
    def _():
        narrow(t, _TB_DIAG)
    gt = gt_ref[0]
    for hh in range(HEADS_PER_GROUP):
        rows = slice(hh * tq, (hh + 1) * tq)
        acc = acc_scr[rows]
        o = acc[:, :D_V] / acc[:, D_V:D_V + 1]
        o_ref[:, hh * D_V:(hh + 1) * D_V] = (o * gt[:, HEADS_PER_GROUP + hh:HEADS_PER_GROUP + hh + 1]).astype(o_ref.dtype)


def _sel_attention(qkv, sel, tb, gates, batch):
    t = qkv.shape[1]
    s = t // batch
    ns = sel.shape[2]
    nt = s // Q_TILE
    g, hg = N_KV_GROUPS, HEADS_PER_GROUP
    assert ns % LANES == 0 and HEAD_PAD == LANES
    assert s % (_FAR_FACTOR * KEY_CHUNK) == 0
    return pl.pallas_call(
        _sel_kernel,
        grid=(batch, g, nt),
        in_specs=[
            pl.BlockSpec((hg, Q_TILE, HEAD_PAD), lambda b, i, j: (i, b * nt + j, 0)),
            pl.BlockSpec((1, s, HEAD_PAD), lambda b, i, j: (SLOT_KS + i, b, 0)),
            pl.BlockSpec((1, s, HEAD_PAD), lambda b, i, j: (SLOT_VS + i, b, 0)),
            pl.BlockSpec((1, Q_TILE, ns), lambda b, i, j: (i, b * nt + j, 0)),
            pl.BlockSpec((1, _TB_SECTIONS, hg * Q_TILE, KEY_CHUNK), lambda b, i, j: (i, 0, 0, 0)),
            pl.BlockSpec((1, Q_TILE, LANES), lambda b, i, j: (i, b * nt + j, 0)),
        ],
        out_specs=pl.BlockSpec((Q_TILE, hg * D_V), lambda b, i, j: (b * nt + j, i)),
        out_shape=jax.ShapeDtypeStruct((t, N_HEADS * D_V), BF16),
        scratch_shapes=[pltpu.VMEM((ns // LANES, hg * Q_TILE, HEAD_PAD + LANES), BF16),
                        pltpu.VMEM((2, hg * Q_TILE, _FAR_FACTOR * KEY_CHUNK), F32),
                        pltpu.VMEM((2, hg * Q_TILE, LANES), F32),
                        pltpu.VMEM((hg * Q_TILE, LANES), F32),
                        pltpu.VMEM((hg * Q_TILE, HEAD_PAD), F32)],
        compiler_params=_params(("parallel", "parallel", "arbitrary"), 60),
        name="sel_attention",
    )(qkv, qkv, qkv, sel, tb, gates)


def _win_kernel(q_ref, k0_ref, k1_ref, k2_ref, v0_ref, v1_ref, v2_ref, tb_ref, mw_ref, gt_ref, o_ref):
    tq = Q_TILE
    t = pl.program_id(2)
    pen0 = jnp.where(t >= 2, 0.0, NEG_INF).astype(F32)
    pen1 = jnp.where(t >= 1, 0.0, NEG_INF).astype(F32)
    q = q_ref[...].reshape(HEADS_PER_GROUP * tq, HEAD_PAD)
    s0 = _dot_nt(q, k0_ref[0]) + (mw_ref[...] + pen0)
    s1 = _dot_nt(q, k1_ref[0]) + (tb_ref[0, _TB_NEAR] + pen1)
    s2 = _dot_nt(q, k2_ref[0]) + tb_ref[0, _TB_DIAG]
    m = jnp.maximum(jnp.maximum(jnp.max(s0, axis=-1, keepdims=True),
                                jnp.max(s1, axis=-1, keepdims=True)),
                    jnp.max(s2, axis=-1, keepdims=True))
    acc = (_dot(jnp.exp2(s0 - m).astype(BF16), v0_ref[0]) + _dot(jnp.exp2(s1 - m).astype(BF16), v1_ref[0])
           + _dot(jnp.exp2(s2 - m).astype(BF16), v2_ref[0]))
    o = acc[:, :D_V] / acc[:, D_V:D_V + 1]
    gt = gt_ref[0]
    for hh in range(HEADS_PER_GROUP):
        col = 2 * HEADS_PER_GROUP + hh
        o_ref[:, hh * D_V:(hh + 1) * D_V] = (o[hh * tq:(hh + 1) * tq] * gt[:, col:col + 1]).astype(o_ref.dtype)


def _win_attention(qkv, tb, gates, batch):
    t = qkv.shape[1]
    s = t // batch
    nt = s // Q_TILE
    g, hg = N_KV_GROUPS, HEADS_PER_GROUP
    assert WINDOW == 2 * KEY_CHUNK and Q_TILE == KEY_CHUNK
    q = np.arange(Q_TILE)[:, None]
    r = np.arange(KEY_CHUNK)[None, :]
    mw = jnp.asarray(np.tile(np.where(r > q, 0.0, NEG_INF).astype(np.float32), (hg, 1)))

    def kv_spec(slot, back):
        return pl.BlockSpec((1, KEY_CHUNK, HEAD_PAD),
                            lambda b, i, j: (slot + i, b * nt + jnp.maximum(j - back, 0), 0))

    return pl.pallas_call(
        _win_kernel,
        grid=(batch, g, nt),
        in_specs=[
            pl.BlockSpec((hg, Q_TILE, HEAD_PAD), lambda b, i, j: (i, b * nt + j, 0)),
            kv_spec(SLOT_KW, 2), kv_spec(SLOT_KW, 1), kv_spec(SLOT_KW, 0),
            kv_spec(SLOT_VW, 2), kv_spec(SLOT_VW, 1), kv_spec(SLOT_VW, 0),
            pl.BlockSpec((1, 2, hg * Q_TILE, KEY_CHUNK), lambda b, i, j: (i, 0, 0, 0)),
            pl.BlockSpec((hg * Q_TILE, KEY_CHUNK), lambda b, i, j: (0, 0)),
            pl.BlockSpec((1, Q_TILE, LANES), lambda b, i, j: (i, b * nt + j, 0)),
        ],
        out_specs=pl.BlockSpec((Q_TILE, hg * D_V), lambda b, i, j: (b * nt + j, i)),
        out_shape=jax.ShapeDtypeStruct((t, N_HEADS * D_V), BF16),
        compiler_params=_params(("parallel", "parallel", "arbitrary"), 48),
        name="win_attention",
    )(qkv, qkv, qkv, qkv, qkv, qkv, qkv, tb, mw, gates)


def _out_proj_kernel(a_ref, b_ref, c_ref, w_ref, x_ref, o_ref, s_scr):
    @pl.when(pl.program_id(1) == 0)
    def _():
        s_scr[...] = (a_ref[...].astype(F32) + b_ref[...].astype(F32) + c_ref[...].astype(F32)).astype(BF16)

    o_ref[...] = x_ref[...] + _dot(s_scr[...], w_ref[...])


def _out_proj(a, b, c, w, x, *, tm=512, tn=1024):
    t, d = a.shape
    n = w.shape[1]
    row = pl.BlockSpec((tm, d), lambda i, j: (i, 0))
    return pl.pallas_call(
        _out_proj_kernel,
        grid=(t // tm, n // tn),
        in_specs=[row, row, row,
                  pl.BlockSpec((d, tn), lambda i, j: (0, j)),
                  pl.BlockSpec((tm, tn), lambda i, j: (i, j))],
        out_specs=pl.BlockSpec((tm, tn), lambda i, j: (i, j)),
        out_shape=jax.ShapeDtypeStruct((t, n), F32),
        scratch_shapes=[pltpu.VMEM((tm, d), BF16)],
        compiler_params=_params(("parallel", "arbitrary"), 48),
        name="attn_out_proj",
    )(a, b, c, w, x)


def _ffn_kernel(x_ref, g_ref, wg_ref, wu_ref, wd_ref, gf_ref, o_ref, h_scr, acc_scr, *, nf, final_norm):
    f = pl.program_id(1)

    @pl.when(f == 0)
    def _():
        h_scr[...] = _rms(x_ref[...], g_ref[...]).astype(BF16)
        acc_scr[...] = jnp.zeros_like(acc_scr)

    h = h_scr[...]
    gate = _dot(h, wg_ref[...])
    up = _dot(h, wu_ref[...])
    act = (jax.nn.silu(gate) * up).astype(BF16)
    acc_scr[...] += _dot(act, wd_ref[...])

    @pl.when(f == nf - 1)
    def _():
        y = x_ref[...] + acc_scr[...]
        if final_norm:
            y = _rms(y, gf_ref[...])
        o_ref[...] = y


def _ffn(x, gamma, w_gu, w_down, gamma_final, *, final_norm, tm=512, tf=512):
    t, d = x.shape
    nf = D_FF // tf
    return pl.pallas_call(
        functools.partial(_ffn_kernel, nf=nf, final_norm=final_norm),
        grid=(t // tm, nf),
        in_specs=[
            pl.BlockSpec((tm, d), lambda i, f: (i, 0)),
            pl.BlockSpec((1, d), lambda i, f: (0, 0)),
            pl.BlockSpec((d, tf), lambda i, f: (0, f)),
            pl.BlockSpec((d, tf), lambda i, f: (0, nf + f)),
            pl.BlockSpec((tf, d), lambda i, f: (f, 0)),
            pl.BlockSpec((1, d), lambda i, f: (0, 0)),
        ],
        out_specs=pl.BlockSpec((tm, d), lambda i, f: (i, 0)),
        out_shape=jax.ShapeDtypeStruct((t, d), F32),
        scratch_shapes=[pltpu.VMEM((tm, d), BF16), pltpu.VMEM((tm, d), F32)],
        compiler_params=_params(("parallel", "arbitrary"), 56),
        name="swiglu_ffn",
    )(x, gamma, w_gu, w_gu, w_down, gamma_final)


_POOL_HIST = 16


def _pool_kernel(u_ref, up_ref, x_ref, wg_ref, sc_ref, wo_ref, o_ref, ext_scr, mix_scr, *, tm, seq):
    i = pl.program_id(0)

    @pl.when(pl.program_id(1) == 0)
    def _():
        t0 = (i * tm) % seq
        hist = up_ref[...]
        ext_scr[0:_POOL_HIST, :] = jnp.where(t0 == 0, jnp.zeros_like(hist), hist)
        ext_scr[_POOL_HIST:, :] = u_ref[...]
        tpos = t0 + lax.broadcasted_iota(jnp.int32, (tm, 1), 0)
        for gi, w in enumerate(POOL_WINDOWS):
            cols = slice(gi * POOL_GROUP, (gi + 1) * POOL_GROUP)
            tot = ext_scr[_POOL_HIST:, cols]
            for back in range(1, w):
                tot = tot + ext_scr[_POOL_HIST - back:_POOL_HIST - back + tm, cols]
            cnt = jnp.minimum(tpos + 1, w).astype(F32)
            pooled = tot / cnt - ext_scr[_POOL_HIST:, cols]
            mixed = _dot(pooled.astype(BF16), wg_ref[gi]) * sc_ref[:, cols]
            mix_scr[:, cols] = mixed.astype(BF16)

    o_ref[...] = x_ref[...] + _dot(mix_scr[...], wo_ref[...])


def _pool_mix(u, x, w_grp, scale, w_out, seq, *, tm=512, tn=1024):
    t, d = u.shape
    hb = tm // _POOL_HIST
    return pl.pallas_call(
        functools.partial(_pool_kernel, tm=tm, seq=seq),
        grid=(t // tm, d // tn),
        in_specs=[
            pl.BlockSpec((tm, d), lambda i, j: (i, 0)),
            pl.BlockSpec((_POOL_HIST, d), lambda i, j: (jnp.maximum(i * hb - 1, 0), 0)),
            pl.BlockSpec((tm, tn), lambda i, j: (i, j)),
            pl.BlockSpec((N_POOL_GROUPS, POOL_GROUP, POOL_GROUP), lambda i, j: (0, 0, 0)),
            pl.BlockSpec((1, d), lambda i, j: (0, 0)),
            pl.BlockSpec((d, tn), lambda i, j: (0, j)),
        ],
        out_specs=pl.BlockSpec((tm, tn), lambda i, j: (i, j)),
        out_shape=jax.ShapeDtypeStruct((t, d), F32),
        scratch_shapes=[pltpu.VMEM((tm + _POOL_HIST, d), F32), pltpu.VMEM((tm, d), BF16)],
        compiler_params=_params(("parallel", "arbitrary"), 56),
        name="pool_mixer",
    )(u, u, x, w_grp, scale, w_out)


def _pad_heads(w, n, dh):
    d = w.shape[0]
    return jnp.pad(w.reshape(d, n, dh), ((0, 0), (0, 0), (0, HEAD_PAD - dh))).reshape(d, n * HEAD_PAD)


def _nsa_weights(w_in, rel_bias):
    g, hg = N_KV_GROUPS, HEADS_PER_GROUP
    qw, kw, vw = N_HEADS * D_K, g * D_K, g * D_V
    offs = np.cumsum([0, qw, kw, vw, kw, vw, kw, vw])
    wq, wkc, wvc, wks, wvs, wkw, wvw = [w_in[:, offs[i]:offs[i + 1]] for i in range(7)]
    wgt = w_in[:, offs[7]:]
    w_heads = jnp.concatenate([
        _pad_heads(wq, N_HEADS, D_K), _pad_heads(wks, g, D_K), _pad_heads(wkw, g, D_K),
        _pad_heads(wvs, g, D_V), _pad_heads(wvw, g, D_V)], axis=1).astype(BF16)
    scale = np.ones((N_SLOTS, HEAD_PAD), np.float32)
    scale[:N_HEADS] = D_K ** -0.5 * LOG2E
    add = np.zeros((N_SLOTS, HEAD_PAD), np.float32)
    add[SLOT_KS:SLOT_VS, D_K] = 1.0
    add[SLOT_VS:, D_V] = 1.0
    add = jnp.asarray(add).at[:N_HEADS, D_K].set(rel_bias[N_BUCKETS - 1].astype(F32) * LOG2E)
    wg = wgt.reshape(-1, 3, g, hg).transpose(0, 2, 1, 3).reshape(-1, g, 3 * hg)
    wg = jnp.pad(wg, ((0, 0), (0, 0), (0, LANES - 3 * hg))).reshape(-1, g * LANES)
    w_side = jnp.concatenate([wkc, wvc, wg], axis=1).astype(BF16)
    return w_heads, jnp.asarray(scale).reshape(1, -1), add.reshape(1, -1), w_side


def _overlap_matrix_t(nc, ns):
    i = np.arange(nc)[None, :]
    j = np.arange(ns)[:, None]
    ratio = SLC_BLOCK // CMP_STRIDE
    diff = i - ratio * j
    w = np.zeros((ns, nc), np.float32)
    for n in range(CMP_BLOCK // CMP_STRIDE):
        w += ((diff + n >= 0) & (diff + n < ratio)).astype(np.float32)
    w[:, nc - 1] = 0.0
    return jnp.asarray(w, dtype=BF16)


def _nsa_layer(x, gamma, rel_bias, w_in, pos_k, pos_v, ck_w1, ck_b1, ck_w2, cv_w1, cv_b1, cv_w2,
               w_out, batch):
    t = x.shape[0]
    seq = t // batch
    assert seq % Q_TILE == 0
    w_heads, sc_row, ad_row, w_side = _nsa_weights(w_in, rel_bias)
    tb, lc = _build_tables(rel_bias)
    qkv = _norm_heads(x, gamma, w_heads, sc_row, ad_row)
    kc, vc, gates = _norm_side(x, gamma, w_side)
    kcmp = _compress(kc, pos_k, ck_w1, ck_b1, ck_w2, D_K, D_K, batch)
    vcmp = _compress(vc, pos_v, cv_w1, cv_b1, cv_w2, D_V, None, batch)
    ovt = _overlap_matrix_t(seq // CMP_STRIDE, seq // SLC_BLOCK)
    o_c, sel = _cmp_attention(qkv, lc, kcmp, vcmp, ovt, gates, batch)
    o_s = _sel_attention(qkv, sel, tb, gates, batch)
    o_w = _win_attention(qkv, tb, gates, batch)
    return _out_proj(o_c, o_s, o_w, w_out.astype(BF16), x)


def kernel(x, norm_mix, norm_ffn, norm_final, rel_bias, nsa_w_in, nsa_pos_k, nsa_pos_v, nsa_ck_w1, nsa_ck_b1, nsa_ck_w2, nsa_cv_w1, nsa_cv_b1, nsa_cv_w2, nsa_w_out, pool_w_in, pool_w_grp, pool_scale, pool_w_out, ffn_w_gu, ffn_w_down):
    batch, seq, d = x.shape
    h = x.reshape(batch * seq, d)
    gfin = norm_final.reshape(1, d)
    h = _nsa_layer(h, norm_mix[0].reshape(1, d), rel_bias, nsa_w_in[0], nsa_pos_k[0], nsa_pos_v[0],
                   nsa_ck_w1[0], nsa_ck_b1[0], nsa_ck_w2[0], nsa_cv_w1[0], nsa_cv_b1[0], nsa_cv_w2[0],
                   nsa_w_out[0], batch)
    h = _ffn(h, norm_ffn[0].reshape(1, d), ffn_w_gu[0].astype(BF16), ffn_w_down[0].astype(BF16), gfin,
             final_norm=False)
    u = _norm_mm(h, norm_mix[1].reshape(1, d), pool_w_in[0].astype(BF16))
    h = _pool_mix(u, h, pool_w_grp[0].astype(BF16), pool_scale[0].reshape(1, d),
                  pool_w_out[0].astype(BF16), seq)
    h = _ffn(h, norm_ffn[1].reshape(1, d), ffn_w_gu[1].astype(BF16), ffn_w_down[1].astype(BF16), gfin,
             final_norm=True)
    return h.reshape(batch, seq, d)
```

```python
import functools
import math

import numpy as np
import jax
import jax.numpy as jnp
from jax import lax
from jax.experimental import pallas as pl
from jax.experimental.pallas import tpu as pltpu

D_MODEL = 2048
N_HEADS = 32
N_KV_GROUPS = 4
HEADS_PER_GROUP = N_HEADS // N_KV_GROUPS
D_K = 96
D_V = 64
CMP_BLOCK = 32
CMP_STRIDE = 16
CMP_HIDDEN = 256
SLC_BLOCK = 64
N_SLC = 16
WINDOW = 512
N_BUCKETS = 32
MAX_DISTANCE = 128
POOL_WINDOWS = (2, 4, 8, 16)
N_POOL_GROUPS = len(POOL_WINDOWS)
POOL_GROUP = D_MODEL // N_POOL_GROUPS
D_FF = -(-8 * D_MODEL // (3 * 256)) * 256
EPS = 1e-6
NEG_INF = -1e30
FORCE_SCORE = 1e9
LOG2E = math.log2(math.e)

LANES = 128
HEAD_PAD = LANES
Q_TILE = 256
KEY_CHUNK = 256
CMP_PER_TILE = Q_TILE // CMP_STRIDE
BIAS_K = CMP_PER_TILE * 8
SLOT_Q = 0
SLOT_KS = N_HEADS
SLOT_KW = SLOT_KS + N_KV_GROUPS
SLOT_VS = SLOT_KW + N_KV_GROUPS
SLOT_VW = SLOT_VS + N_KV_GROUPS
N_SLOTS = SLOT_VW + N_KV_GROUPS

F32 = jnp.float32
BF16 = jnp.bfloat16


def _bucket_np(dist):
    n = np.maximum(dist, 0)
    max_exact = N_BUCKETS // 2
    nf = np.maximum(n, 1).astype(np.float32)
    scaled = (np.log(nf / np.float32(max_exact)) / np.float32(math.log(MAX_DISTANCE / max_exact))
              * np.float32(N_BUCKETS - max_exact)).astype(np.float32)
    large = np.minimum(max_exact + scaled.astype(np.int32), N_BUCKETS - 1)
    return np.where(n < max_exact, n, large).astype(np.int32)


_FAR_DIST = int(np.max(np.nonzero(_bucket_np(np.arange(4096)) != N_BUCKETS - 1)[0])) + 1
assert _FAR_DIST <= CMP_STRIDE * 8 + (CMP_STRIDE - 1) - (CMP_BLOCK - 1) + 1 and _FAR_DIST <= KEY_CHUNK
_MASKED_BUCKET = N_BUCKETS
_ZERO_BUCKET = N_BUCKETS + 1


def _toeplitz_bucket_map():
    q = np.arange(Q_TILE)[:, None]
    r = np.arange(2 * KEY_CHUNK)[None, :]
    d = q + KEY_CHUNK - r
    b = _bucket_np(d)
    b = np.where(d >= _FAR_DIST, _ZERO_BUCKET, b)
    return np.where(d < 0, _MASKED_BUCKET, b).astype(np.int32)


def _cmp_bucket_map():
    q = np.arange(Q_TILE)[:, None]
    k = np.arange(BIAS_K)[None, :]
    a, b = q // CMP_STRIDE, q % CMP_STRIDE
    a2, m = k // 8, k % 8 + 1
    d = CMP_STRIDE * m + b - (CMP_BLOCK - 1)
    bk = _bucket_np(d)
    bk = np.where((d >= _FAR_DIST) | (d < 0) | (a != a2), _ZERO_BUCKET, bk)
    return bk.astype(np.int32)


def _params(sem, vmem_mb):
    return pltpu.CompilerParams(dimension_semantics=sem, vmem_limit_bytes=vmem_mb * 1024 * 1024)


def _rms(x, g):
    ms = jnp.mean(x * x, axis=-1, keepdims=True)
    return x * lax.rsqrt(ms + EPS) * g


def _dot(a, b):
    return jnp.dot(a, b, preferred_element_type=F32)


def _dot_nt(a, b):
    return lax.dot_general(a, b, (((1,), (1,)), ((), ())), preferred_element_type=F32)


def _split_rows(dot, a, b):
    half = a.shape[0] // 2
    return jnp.concatenate([dot(a[:half], b), dot(a[half:], b)], axis=0)


def _tables_kernel(tbl_ref, bm_ref, bc_ref, tb_ref, lc_ref):
    h = pl.program_id(0) * HEADS_PER_GROUP + pl.program_id(1)
    far = tbl_ref[N_BUCKETS - 1, h]
    bm = bm_ref[...]
    bc = bc_ref[...]
    tb = jnp.where(bm == _MASKED_BUCKET, NEG_INF, 0.0).astype(F32)
    lc = jnp.zeros(bc.shape, F32)
    for b in range(N_BUCKETS - 1):
        val = (tbl_ref[b, h] - far) * LOG2E
        tb = jnp.where(bm == b, val, tb)
        lc = jnp.where(bc == b, val, lc)
    tb_ref[0, _TB_NEAR] = tb[:, :KEY_CHUNK]
    tb_ref[0, _TB_DIAG] = tb[:, KEY_CHUNK:]
    lc_ref[0] = lc.astype(BF16)


_TB_NEAR, _TB_DIAG = 0, 1
_TB_SECTIONS = 2


def _build_tables(rel_bias):
    bm = jnp.asarray(_toeplitz_bucket_map())
    bc = jnp.asarray(_cmp_bucket_map())
    g, hg = N_KV_GROUPS, HEADS_PER_GROUP
    return pl.pallas_call(
        _tables_kernel,
        grid=(g, hg),
        in_specs=[
            pl.BlockSpec(memory_space=pltpu.SMEM),
            pl.BlockSpec(bm.shape, lambda i, j: (0, 0)),
            pl.BlockSpec(bc.shape, lambda i, j: (0, 0)),
        ],
        out_specs=[
            pl.BlockSpec((1, _TB_SECTIONS, Q_TILE, KEY_CHUNK), lambda i, j: (i, 0, j, 0)),
            pl.BlockSpec((1, Q_TILE, BIAS_K), lambda i, j: (i, j, 0)),
        ],
        out_shape=[
            jax.ShapeDtypeStruct((g, _TB_SECTIONS, hg * Q_TILE, KEY_CHUNK), F32),
            jax.ShapeDtypeStruct((g, hg * Q_TILE, BIAS_K), BF16),
        ],
        compiler_params=_params(("arbitrary", "arbitrary"), 32),
        name="bias_tables",
    )(rel_bias, bm, bc)


def _norm_heads_kernel(x_ref, g_ref, w_ref, sc_ref, ad_ref, o_ref, h_scr, *, slabs):
    @pl.when(pl.program_id(1) == 0)
    def _():
        h_scr[...] = _rms(x_ref[...], g_ref[...]).astype(BF16)

    acc = _dot(h_scr[...], w_ref[...]) * sc_ref[...] + ad_ref[...]
    for k in range(slabs):
        o_ref[k] = acc[:, k * HEAD_PAD:(k + 1) * HEAD_PAD].astype(o_ref.dtype)


def _norm_heads(x, gamma, w, scale_row, add_row, *, tm=512, slabs=8):
    t, d = x.shape
    n = w.shape[1]
    tn = slabs * HEAD_PAD
    return pl.pallas_call(
        functools.partial(_norm_heads_kernel, slabs=slabs),
        grid=(t // tm, n // tn),
        in_specs=[
            pl.BlockSpec((tm, d), lambda i, j: (i, 0)),
            pl.BlockSpec((1, d), lambda i, j: (0, 0)),
            pl.BlockSpec((d, tn), lambda i, j: (0, j)),
            pl.BlockSpec((1, tn), lambda i, j: (0, j)),
            pl.BlockSpec((1, tn), lambda i, j: (0, j)),
        ],
        out_specs=pl.BlockSpec((slabs, tm, HEAD_PAD), lambda i, j: (j, i, 0)),
        out_shape=jax.ShapeDtypeStruct((n // HEAD_PAD, t, HEAD_PAD), BF16),
        scratch_shapes=[pltpu.VMEM((tm, d), BF16)],
        compiler_params=_params(("parallel", "arbitrary"), 48),
        name="norm_proj_heads",
    )(x, gamma, w, scale_row, add_row)


def _norm_side_kernel(x_ref, g_ref, w_ref, kc_ref, vc_ref, gt_ref, *, kw, vw):
    h = _rms(x_ref[...], g_ref[...]).astype(BF16)
    acc = _dot(h, w_ref[...])
    kc_ref[...] = acc[:, :kw]
    vc_ref[...] = acc[:, kw:kw + vw]
    for g in range(N_KV_GROUPS):
        lo = kw + vw + g * LANES
        gt_ref[g] = jax.nn.sigmoid(acc[:, lo:lo + LANES])


def _norm_side(x, gamma, w, *, tm=512):
    t, d = x.shape
    kw, vw = N_KV_GROUPS * D_K, N_KV_GROUPS * D_V
    n = w.shape[1]
    return pl.pallas_call(
        functools.partial(_norm_side_kernel, kw=kw, vw=vw),
        grid=(t // tm,),
        in_specs=[
            pl.BlockSpec((tm, d), lambda i: (i, 0)),
            pl.BlockSpec((1, d), lambda i: (0, 0)),
            pl.BlockSpec((d, n), lambda i: (0, 0)),
        ],
        out_specs=[
            pl.BlockSpec((tm, kw), lambda i: (i, 0)),
            pl.BlockSpec((tm, vw), lambda i: (i, 0)),
            pl.BlockSpec((N_KV_GROUPS, tm, LANES), lambda i: (0, i, 0)),
        ],
        out_shape=[
            jax.ShapeDtypeStruct((t, kw), F32),
            jax.ShapeDtypeStruct((t, vw), F32),
            jax.ShapeDtypeStruct((N_KV_GROUPS, t, LANES), F32),
        ],
        compiler_params=_params(("parallel",), 48),
        name="norm_proj_side",
    )(x, gamma, w)


def _norm_mm_kernel(x_ref, g_ref, w_ref, o_ref, h_scr):
    @pl.when(pl.program_id(1) == 0)
    def _():
        h_scr[...] = _rms(x_ref[...], g_ref[...]).astype(BF16)

    o_ref[...] = _dot(h_scr[...], w_ref[...])


def _norm_mm(x, gamma, w, *, tm=512, tn=1024):
    t, d = x.shape
    n = w.shape[1]
    return pl.pallas_call(
        _norm_mm_kernel,
        grid=(t // tm, n // tn),
        in_specs=[
            pl.BlockSpec((tm, d), lambda i, j: (i, 0)),
            pl.BlockSpec((1, d), lambda i, j: (0, 0)),
            pl.BlockSpec((d, tn), lambda i, j: (0, j)),
        ],
        out_specs=pl.BlockSpec((tm, tn), lambda i, j: (i, j)),
        out_shape=jax.ShapeDtypeStruct((t, n), F32),
        scratch_shapes=[pltpu.VMEM((tm, d), BF16)],
        compiler_params=_params(("parallel", "arbitrary"), 48),
        name="norm_matmul",
    )(x, gamma, w)


def _compress_kernel(x_ref, pa_ref, pb_ref, wa_ref, wb_ref, b1_ref, w2_ref, ad_ref, o_ref,
                     acca, accb, *, nk):
    k = pl.program_id(1)

    @pl.when(k == 0)
    def _():
        acca[...] = jnp.zeros_like(acca)
        accb[...] = jnp.zeros_like(accb)

    x = x_ref[...]
    acca[...] += _dot((x + pa_ref[...]).astype(BF16), wa_ref[...])
    accb[...] += _dot((x + pb_ref[...]).astype(BF16), wb_ref[...])

    @pl.when(k == nk - 1)
    def _():
        rows = accb.shape[0]
        hid = acca[...] + pltpu.roll(accb[...], rows - 1, 0) + b1_ref[...]
        hid = jax.nn.gelu(hid)
        out = _dot(hid.astype(BF16), w2_ref[...]) + ad_ref[...]
        for g in range(N_KV_GROUPS):
            o_ref[g] = out[:, g * HEAD_PAD:(g + 1) * HEAD_PAD].astype(o_ref.dtype)


def _compress(kv, pos, w1, b1, w2, dh, one_col, batch):
    t = kv.shape[0]
    g = N_KV_GROUPS
    chunks = t // CMP_STRIDE
    rows = chunks // batch
    width = CMP_STRIDE * g * dh
    x = kv.reshape(chunks, width)
    eye = jnp.eye(g, dtype=F32)
    w1r = w1.reshape(2, CMP_STRIDE, dh, CMP_HIDDEN)

    def expand(wh):
        return jnp.einsum('rdj,gh->rgdhj', wh, eye).reshape(width, g * CMP_HIDDEN).astype(BF16)

    wa, wb = expand(w1r[0]), expand(w1r[1])
    posr = pos.reshape(2, CMP_STRIDE, 1, dh)
    pa = jnp.broadcast_to(posr[0], (CMP_STRIDE, g, dh)).reshape(1, width)
    pb = jnp.broadcast_to(posr[1], (CMP_STRIDE, g, dh)).reshape(1, width)
    b1r = jnp.tile(b1.reshape(1, CMP_HIDDEN), (1, g))
    w2p = jnp.pad(w2, ((0, 0), (0, HEAD_PAD - dh)))
    w2x = jnp.einsum('jd,gh->gjhd', w2p, eye).reshape(g * CMP_HIDDEN, g * HEAD_PAD).astype(BF16)
    ad = np.zeros((1, g * HEAD_PAD), np.float32)
    if one_col is not None:
        ad[0, one_col::HEAD_PAD] = 1.0
    ad = jnp.asarray(ad)
    kt = width // 8
    nk = width // kt
    return pl.pallas_call(
        functools.partial(_compress_kernel, nk=nk),
        grid=(batch, nk),
        in_specs=[
            pl.BlockSpec((rows, kt), lambda b, k: (b, k)),
            pl.BlockSpec((1, kt), lambda b, k: (0, k)),
            pl.BlockSpec((1, kt), lambda b, k: (0, k)),
            pl.BlockSpec((kt, g * CMP_HIDDEN), lambda b, k: (k, 0)),
            pl.BlockSpec((kt, g * CMP_HIDDEN), lambda b, k: (k, 0)),
            pl.BlockSpec((1, g * CMP_HIDDEN), lambda b, k: (0, 0)),
            pl.BlockSpec((g * CMP_HIDDEN, g * HEAD_PAD), lambda b, k: (0, 0)),
            pl.BlockSpec((1, g * HEAD_PAD), lambda b, k: (0, 0)),
        ],
        out_specs=pl.BlockSpec((g, rows, HEAD_PAD), lambda b, k: (0, b, 0)),
        out_shape=jax.ShapeDtypeStruct((g, chunks, HEAD_PAD), BF16),
        scratch_shapes=[pltpu.VMEM((rows, g * CMP_HIDDEN), F32),
                        pltpu.VMEM((rows, g * CMP_HIDDEN), F32)],
        compiler_params=_params(("parallel", "arbitrary"), 48),
        name="compress_mlp",
    )(x, pa, pb, wa, wb, b1r, w2x, ad)


_CMP_MASK_LANE = D_K + 1
_CMP_THR_MIN = -((CMP_BLOCK - 1 + CMP_STRIDE - 1) // CMP_STRIDE)
_CMP_MASK_N = (Q_TILE - CMP_BLOCK) // CMP_STRIDE - _CMP_THR_MIN + 1
assert _CMP_MASK_LANE + _CMP_MASK_N <= HEAD_PAD


def _cmp_query_mask():
    q = np.arange(Q_TILE)
    thr = (q - (CMP_BLOCK - 1)) // CMP_STRIDE
    m = np.zeros((Q_TILE, HEAD_PAD), np.float32)
    m[q, _CMP_MASK_LANE + thr - _CMP_THR_MIN] = 1.0
    return jnp.asarray(np.tile(m, (HEADS_PER_GROUP, 1)), dtype=BF16)


def _cmp_kernel(q_ref, qm_ref, lc_ref, kc_ref, vc_ref, ovt_ref, gt_ref, o_ref, sel_ref,
                s_scr, p_scr, psum_scr, selt_scr, *, n_sel):
    tq = Q_TILE
    t = pl.program_id(2)
    t0 = t * tq
    nc = kc_ref.shape[1]
    ns = ovt_ref.shape[0]
    ii = lax.broadcasted_iota(jnp.int32, (nc, LANES), 0) - CMP_PER_TILE * t
    ll = lax.broadcasted_iota(jnp.int32, (nc, LANES), 1)
    hot = ii == ((ll >> 3) - ((ll & 7) + 1))
    late = ((ll >= _CMP_MASK_LANE) & (ll < _CMP_MASK_LANE + _CMP_MASK_N)
            & (ii > ll - (_CMP_MASK_LANE - _CMP_THR_MIN)))
    keys = jnp.where(late, NEG_INF, kc_ref[0].astype(F32)).astype(BF16)
    rhs = jnp.concatenate([keys, jnp.where(hot, 1.0, 0.0).astype(BF16)], axis=1)
    q = q_ref[...].reshape(HEADS_PER_GROUP * tq, HEAD_PAD) + qm_ref[...]
    lhs = jnp.concatenate([q, lc_ref[0]], axis=1)
    gt = gt_ref[0]
    n_chunk = max(nc // KEY_CHUNK, 1)
    wchunk = nc // n_chunk
    live = jnp.minimum((t0 + tq - CMP_BLOCK) // (CMP_STRIDE * wchunk) + 1, n_chunk)

    def attend(w):
        s_scr[:, :w] = _split_rows(_dot_nt, lhs, rhs[:w])
        for hh in range(HEADS_PER_GROUP):
            rows = slice(hh * tq, (hh + 1) * tq)
            s = s_scr[rows, :w]
            m = jnp.max(s, axis=-1, keepdims=True)
            e = jnp.exp2(s - m)
            l = jnp.sum(e, axis=-1, keepdims=True)
            p = e * jnp.where(m > 0.5 * NEG_INF, 1.0 / l, 0.0)
            if hh == 0:
                psum_scr[:, :w] = p
            else:
                psum_scr[:, :w] += p
            p_scr[rows, :w] = p.astype(BF16)
        o = _split_rows(_dot, p_scr[:, :w], vc_ref[0, :w, :])
        for hh in range(HEADS_PER_GROUP):
            o_ref[:, hh * D_V:(hh + 1) * D_V] = (o[hh * tq:(hh + 1) * tq, :D_V] * gt[:, hh:hh + 1]).astype(o_ref.dtype)
        nr = min(ns, w * CMP_STRIDE // SLC_BLOCK)
        psum = psum_scr[:, :w]
        p_hi = psum.astype(BF16)
        p_lo = (psum - p_hi.astype(F32)).astype(BF16)
        ovt = ovt_ref[:nr, :w]
        imp = _dot_nt(ovt, p_hi) + _dot_nt(ovt, p_lo)
        jr = lax.broadcasted_iota(jnp.int32, (nr, tq), 0)
        cur = (t0 + lax.broadcasted_iota(jnp.int32, (nr, tq), 1)) // SLC_BLOCK
        forced = (jr == 0) | (jr == cur) | (jr == cur - 1)
        n_forced = 3
        distinct = (t0 // SLC_BLOCK >= 2) & (n_sel > n_forced)
        imp = jnp.where(forced, jnp.where(distinct, -jnp.inf, FORCE_SCORE), imp)
        imp = jnp.where(jr > cur, -1.0, imp)
        jrf = jr.astype(F32)

        def pick(_, imp):
            m = jnp.max(imp, axis=0, keepdims=True)
            first = jnp.min(jnp.where(imp == m, jrf, 1e9), axis=0, keepdims=True)
            return jnp.where(jrf == first, -jnp.inf, imp)

        imp = lax.fori_loop(0, n_sel - jnp.where(distinct, n_forced, 0), pick, imp)
        selt_scr[0:nr] = jnp.where(imp == -jnp.inf, 1.0, 0.0)
        if nr < ns:
            selt_scr[nr:] = jnp.zeros((ns - nr, tq), F32)

    for k in range(1, n_chunk + 1):
        pl.when(live == k)(functools.partial(attend, k * wchunk))

    sel = selt_scr[...].T.astype(BF16)
    if sel_ref.shape[2] == ns:
        sel_ref[0] = sel
    else:
        sel_ref[0] = jnp.zeros(sel_ref.shape[1:], BF16)
        sel_ref[0, :, 0:ns] = sel


def _cmp_attention(qkv, lc, kcmp, vcmp, ovt, gates, batch):
    t = qkv.shape[1]
    s = t // batch
    nc = kcmp.shape[1] // batch
    ns = ovt.shape[0]
    nt = s // Q_TILE
    g, hg = N_KV_GROUPS, HEADS_PER_GROUP
    ns_pad = -(-ns // LANES) * LANES
    return pl.pallas_call(
        functools.partial(_cmp_kernel, n_sel=min(N_SLC, ns)),
        grid=(batch, g, nt),
        in_specs=[
            pl.BlockSpec((hg, Q_TILE, HEAD_PAD), lambda b, i, j: (i, b * nt + j, 0)),
            pl.BlockSpec((hg * Q_TILE, HEAD_PAD), lambda b, i, j: (0, 0)),
            pl.BlockSpec((1, hg * Q_TILE, BIAS_K), lambda b, i, j: (i, 0, 0)),
            pl.BlockSpec((1, nc, HEAD_PAD), lambda b, i, j: (i, b, 0)),
            pl.BlockSpec((1, nc, HEAD_PAD), lambda b, i, j: (i, b, 0)),
            pl.BlockSpec((ns, nc), lambda b, i, j: (0, 0)),
            pl.BlockSpec((1, Q_TILE, LANES), lambda b, i, j: (i, b * nt + j, 0)),
        ],
        out_specs=[
            pl.BlockSpec((Q_TILE, hg * D_V), lambda b, i, j: (b * nt + j, i)),
            pl.BlockSpec((1, Q_TILE, ns_pad), lambda b, i, j: (i, b * nt + j, 0)),
        ],
        out_shape=[
            jax.ShapeDtypeStruct((t, N_HEADS * D_V), BF16),
            jax.ShapeDtypeStruct((g, t, ns_pad), BF16),
        ],
        scratch_shapes=[pltpu.VMEM((hg * Q_TILE, nc), F32), pltpu.VMEM((hg * Q_TILE, nc), BF16),
                        pltpu.VMEM((Q_TILE, nc), F32), pltpu.VMEM((ns, Q_TILE), F32)],
        compiler_params=_params(("parallel", "parallel", "arbitrary"), 48),
        name="cmp_attention",
    )(qkv, _cmp_query_mask(), lc, kcmp, vcmp, ovt, gates)


_FAR_FACTOR = 4


def _sel_kernel(q_ref, ks_ref, vs_ref, sel_ref, tb_ref, gt_ref, o_ref,
                lhs_scr, s_scr, smax_scr, m_scr, acc_scr):
    tq = Q_TILE
    t = pl.program_id(2)
    n_var = sel_ref.shape[2] // LANES
    chunks_per_var = LANES * SLC_BLOCK // KEY_CHUNK
    blocks_per_chunk = KEY_CHUNK // SLC_BLOCK
    unsel = (1.0 - sel_ref[0].astype(F32)).astype(BF16)
    def build_lhs(var):
        for hh in range(HEADS_PER_GROUP):
            rows = slice(hh * tq, (hh + 1) * tq)
            lhs_scr[var, rows, 0:HEAD_PAD] = q_ref[hh]
            lhs_scr[var, rows, HEAD_PAD:] = unsel[:, var * LANES:(var + 1) * LANES]

    build_lhs(0)
    for var in range(1, n_var):
        pl.when(t >= var * chunks_per_var)(functools.partial(build_lhs, var))
    m_scr[...] = jnp.full(m_scr.shape, NEG_INF, F32)
    acc_scr[...] = jnp.zeros(acc_scr.shape, F32)
    lane = lax.broadcasted_iota(jnp.int32, (KEY_CHUNK, LANES), 1)
    kblk = lax.broadcasted_iota(jnp.int32, (KEY_CHUNK, LANES), 0) // SLC_BLOCK

    n_far = jnp.maximum(t - 1, 0)
    n_wide = n_far // _FAR_FACTOR
    wide = _FAR_FACTOR * KEY_CHUNK
    lane_w = lax.broadcasted_iota(jnp.int32, (wide, LANES), 1)
    kblk_w = lax.broadcasted_iota(jnp.int32, (wide, LANES), 0) // SLC_BLOCK

    def qk(start, nkeys, first_block, phantom):
        k = ks_ref[0, pl.ds(start, nkeys), :]
        flag = first_block % LANES + (kblk if nkeys == KEY_CHUNK else kblk_w)
        armed = (lane if nkeys == KEY_CHUNK else lane_w) == flag
        if phantom is not None:
            armed = armed | phantom
        rhs = jnp.concatenate([k, jnp.where(armed, NEG_INF, 0.0).astype(BF16)], axis=1)
        return _dot_nt(lhs_scr[first_block // LANES], rhs)

    def accumulate(start, nkeys, s, smax):
        v = vs_ref[0, pl.ds(start, nkeys), :]
        m_old = m_scr[...]
        m_new = jnp.maximum(m_old, smax)
        alpha = jnp.exp2(m_old - m_new)
        p = jnp.exp2(s - jnp.concatenate([m_new] * (nkeys // LANES), axis=1))
        acc_scr[...] = alpha * acc_scr[...] + _dot(p.astype(BF16), v)
        m_scr[...] = m_new

    def wide_start(i):
        return pl.multiple_of(jnp.minimum(i, t // _FAR_FACTOR) * wide, wide)

    def scores(i, buf):
        ii = jnp.minimum(i, t // _FAR_FACTOR)
        s = qk(wide_start(i), wide, ii * (wide // SLC_BLOCK), i >= n_wide)
        s_scr[buf] = s
        smax_scr[buf] = jnp.broadcast_to(jnp.max(s, axis=-1, keepdims=True), smax_scr.shape[1:])

    def update(i, buf):
        accumulate(wide_start(i), wide, s_scr[buf], smax_scr[buf])

    def narrow(c, section):
        start = pl.multiple_of(c * KEY_CHUNK, KEY_CHUNK)
        s = qk(start, KEY_CHUNK, c * blocks_per_chunk, None)
        if section is not None:
            s = s + tb_ref[0, section]
        accumulate(start, KEY_CHUNK, s, jnp.max(s, axis=-1, keepdims=True))

    scores(0, 0)

    def trip(_, base):
        for u in range(2):
            scores(base + u + 1, (u + 1) % 2)
            update(base + u, u % 2)
        return base + 2

    lax.fori_loop(0, n_wide // 2, trip, 0)

    @pl.when(n_wide % 2 == 1)
    def _():
        update(n_wide - 1, 0)

    for left in range(_FAR_FACTOR):
        @pl.when((t >= 1) & (n_far % _FAR_FACTOR == left))
        def _():
            for u in range(left):
                narrow(n_wide * _FAR_FACTOR + u, None)
            narrow(t - 1, _TB_NEAR)
            narrow(t, _TB_DIAG)

    @pl.when(t == 0)
    def _():
        narrow(t, _TB_DIAG)
    gt = gt_ref[0]
    for hh in range(HEADS_PER_GROUP):
        rows = slice(hh * tq, (hh + 1) * tq)
        acc = acc_scr[rows]
        o = acc[:, :D_V] / acc[:, D_V:D_V + 1]
        o_ref[:, hh * D_V:(hh + 1) * D_V] = (o * gt[:, HEADS_PER_GROUP + hh:HEADS_PER_GROUP + hh + 1]).astype(o_ref.dtype)


def _sel_attention(qkv, sel, tb, gates, batch):
    t = qkv.shape[1]
    s = t // batch
    ns = sel.shape[2]
    nt = s // Q_TILE
    g, hg = N_KV_GROUPS, HEADS_PER_GROUP
    assert ns % LANES == 0 and HEAD_PAD == LANES
    assert s % (_FAR_FACTOR * KEY_CHUNK) == 0
    return pl.pallas_call(
        _sel_kernel,
        grid=(batch, g, nt),
        in_specs=[
            pl.BlockSpec((hg, Q_TILE, HEAD_PAD), lambda b, i, j: (i, b * nt + j, 0)),
            pl.BlockSpec((1, s, HEAD_PAD), lambda b, i, j: (SLOT_KS + i, b, 0)),
            pl.BlockSpec((1, s, HEAD_PAD), lambda b, i, j: (SLOT_VS + i, b, 0)),
            pl.BlockSpec((1, Q_TILE, ns), lambda b, i, j: (i, b * nt + j, 0)),
            pl.BlockSpec((1, _TB_SECTIONS, hg * Q_TILE, KEY_CHUNK), lambda b, i, j: (i, 0, 0, 0)),
            pl.BlockSpec((1, Q_TILE, LANES), lambda b, i, j: (i, b * nt + j, 0)),
        ],
        out_specs=pl.BlockSpec((Q_TILE, hg * D_V), lambda b, i, j: (b * nt + j, i)),
        out_shape=jax.ShapeDtypeStruct((t, N_HEADS * D_V), BF16),
        scratch_shapes=[pltpu.VMEM((ns // LANES, hg * Q_TILE, HEAD_PAD + LANES), BF16),
                        pltpu.VMEM((2, hg * Q_TILE, _FAR_FACTOR * KEY_CHUNK), F32),
                        pltpu.VMEM((2, hg * Q_TILE, LANES), F32),
                        pltpu.VMEM((hg * Q_TILE, LANES), F32),
                        pltpu.VMEM((hg * Q_TILE, HEAD_PAD), F32)],
        compiler_params=_params(("parallel", "parallel", "arbitrary"), 60),
        name="sel_attention",
    )(qkv, qkv, qkv, sel, tb, gates)


def _win_kernel(q_ref, k0_ref, k1_ref, k2_ref, v0_ref, v1_ref, v2_ref, tb_ref, mw_ref, gt_ref, o_ref):
    tq = Q_TILE
    t = pl.program_id(2)
    pen0 = jnp.where(t >= 2, 0.0, NEG_INF).astype(F32)
    pen1 = jnp.where(t >= 1, 0.0, NEG_INF).astype(F32)
    q = q_ref[...].reshape(HEADS_PER_GROUP * tq, HEAD_PAD)
    s0 = _split_rows(_dot_nt, q, k0_ref[0]) + (mw_ref[...] + pen0)
    s1 = _split_rows(_dot_nt, q, k1_ref[0]) + (tb_ref[0, _TB_NEAR] + pen1)
    s2 = _split_rows(_dot_nt, q, k2_ref[0]) + tb_ref[0, _TB_DIAG]
    m = jnp.maximum(jnp.maximum(jnp.max(s0, axis=-1, keepdims=True),
                                jnp.max(s1, axis=-1, keepdims=True)),
                    jnp.max(s2, axis=-1, keepdims=True))
    acc = (_split_rows(_dot, jnp.exp2(s0 - m).astype(BF16), v0_ref[0])
           + _split_rows(_dot, jnp.exp2(s1 - m).astype(BF16), v1_ref[0])
           + _split_rows(_dot, jnp.exp2(s2 - m).astype(BF16), v2_ref[0]))
    o = acc[:, :D_V] / acc[:, D_V:D_V + 1]
    gt = gt_ref[0]
    for hh in range(HEADS_PER_GROUP):
        col = 2 * HEADS_PER_GROUP + hh
        o_ref[:, hh * D_V:(hh + 1) * D_V] = (o[hh * tq:(hh + 1) * tq] * gt[:, col:col + 1]).astype(o_ref.dtype)


def _win_attention(qkv, tb, gates, batch):
    t = qkv.shape[1]
    s = t // batch
    nt = s // Q_TILE
    g, hg = N_KV_GROUPS, HEADS_PER_GROUP
    assert WINDOW == 2 * KEY_CHUNK and Q_TILE == KEY_CHUNK
    q = np.arange(Q_TILE)[:, None]
    r = np.arange(KEY_CHUNK)[None, :]
    mw = jnp.asarray(np.tile(np.where(r > q, 0.0, NEG_INF).astype(np.float32), (hg, 1)))

    def kv_spec(slot, back):
        return pl.BlockSpec((1, KEY_CHUNK, HEAD_PAD),
                            lambda b, i, j: (slot + i, b * nt + jnp.maximum(j - back, 0), 0))

    return pl.pallas_call(
        _win_kernel,
        grid=(batch, g, nt),
        in_specs=[
            pl.BlockSpec((hg, Q_TILE, HEAD_PAD), lambda b, i, j: (i, b * nt + j, 0)),
            kv_spec(SLOT_KW, 2), kv_spec(SLOT_KW, 1), kv_spec(SLOT_KW, 0),
            kv_spec(SLOT_VW, 2), kv_spec(SLOT_VW, 1), kv_spec(SLOT_VW, 0),
            pl.BlockSpec((1, 2, hg * Q_TILE, KEY_CHUNK), lambda b, i, j: (i, 0, 0, 0)),
            pl.BlockSpec((hg * Q_TILE, KEY_CHUNK), lambda b, i, j: (0, 0)),
            pl.BlockSpec((1, Q_TILE, LANES), lambda b, i, j: (i, b * nt + j, 0)),
        ],
        out_specs=pl.BlockSpec((Q_TILE, hg * D_V), lambda b, i, j: (b * nt + j, i)),
        out_shape=jax.ShapeDtypeStruct((t, N_HEADS * D_V), BF16),
        compiler_params=_params(("parallel", "parallel", "arbitrary"), 48),
        name="win_attention",
    )(qkv, qkv, qkv, qkv, qkv, qkv, qkv, tb, mw, gates)


def _out_proj_kernel(a_ref, b_ref, c_ref, w_ref, x_ref, o_ref, s_scr):
    @pl.when(pl.program_id(1) == 0)
    def _():
        s_scr[...] = (a_ref[...].astype(F32) + b_ref[...].astype(F32) + c_ref[...].astype(F32)).astype(BF16)

    o_ref[...] = x_ref[...] + _dot(s_scr[...], w_ref[...])


def _out_proj(a, b, c, w, x, *, tm=512, tn=1024):
    t, d = a.shape
    n = w.shape[1]
    row = pl.BlockSpec((tm, d), lambda i, j: (i, 0))
    return pl.pallas_call(
        _out_proj_kernel,
        grid=(t // tm, n // tn),
        in_specs=[row, row, row,
                  pl.BlockSpec((d, tn), lambda i, j: (0, j)),
                  pl.BlockSpec((tm, tn), lambda i, j: (i, j))],
        out_specs=pl.BlockSpec((tm, tn), lambda i, j: (i, j)),
        out_shape=jax.ShapeDtypeStruct((t, n), F32),
        scratch_shapes=[pltpu.VMEM((tm, d), BF16)],
        compiler_params=_params(("parallel", "arbitrary"), 48),
        name="attn_out_proj",
    )(a, b, c, w, x)


def _ffn_kernel(x_ref, g_ref, wg_ref, wu_ref, wd_ref, gf_ref, o_ref, h_scr, acc_scr, *, nf, final_norm):
    f = pl.program_id(1)

    @pl.when(f == 0)
    def _():
        h_scr[...] = _rms(x_ref[...], g_ref[...]).astype(BF16)
        acc_scr[...] = jnp.zeros_like(acc_scr)

    h = h_scr[...]
    gate = _dot(h, wg_ref[...])
    up = _dot(h, wu_ref[...])
    act = (jax.nn.silu(gate) * up).astype(BF16)
    acc_scr[...] += _dot(act, wd_ref[...])

    @pl.when(f == nf - 1)
    def _():
        y = x_ref[...] + acc_scr[...]
        if final_norm:
            y = _rms(y, gf_ref[...])
        o_ref[...] = y


def _ffn(x, gamma, w_gu, w_down, gamma_final, *, final_norm, tm=512, tf=512):
    t, d = x.shape
    nf = D_FF // tf
    return pl.pallas_call(
        functools.partial(_ffn_kernel, nf=nf, final_norm=final_norm),
        grid=(t // tm, nf),
        in_specs=[
            pl.BlockSpec((tm, d), lambda i, f: (i, 0)),
            pl.BlockSpec((1, d), lambda i, f: (0, 0)),
            pl.BlockSpec((d, tf), lambda i, f: (0, f)),
            pl.BlockSpec((d, tf), lambda i, f: (0, nf + f)),
            pl.BlockSpec((tf, d), lambda i, f: (f, 0)),
            pl.BlockSpec((1, d), lambda i, f: (0, 0)),
        ],
        out_specs=pl.BlockSpec((tm, d), lambda i, f: (i, 0)),
        out_shape=jax.ShapeDtypeStruct((t, d), F32),
        scratch_shapes=[pltpu.VMEM((tm, d), BF16), pltpu.VMEM((tm, d), F32)],
        compiler_params=_params(("parallel", "arbitrary"), 56),
        name="swiglu_ffn",
    )(x, gamma, w_gu, w_gu, w_down, gamma_final)


_POOL_HIST = 16


def _pool_kernel(u_ref, up_ref, x_ref, wg_ref, sc_ref, wo_ref, o_ref, ext_scr, mix_scr, *, tm, seq):
    i = pl.program_id(0)

    @pl.when(pl.program_id(1) == 0)
    def _():
        t0 = (i * tm) % seq
        hist = up_ref[...]
        ext_scr[0:_POOL_HIST, :] = jnp.where(t0 == 0, jnp.zeros_like(hist), hist)
        ext_scr[_POOL_HIST:, :] = u_ref[...]
        tpos = t0 + lax.broadcasted_iota(jnp.int32, (tm, 1), 0)
        for gi, w in enumerate(POOL_WINDOWS):
            cols = slice(gi * POOL_GROUP, (gi + 1) * POOL_GROUP)
            tot = ext_scr[_POOL_HIST:, cols]
            for back in range(1, w):
                tot = tot + ext_scr[_POOL_HIST - back:_POOL_HIST - back + tm, cols]
            cnt = jnp.minimum(tpos + 1, w).astype(F32)
            pooled = tot / cnt - ext_scr[_POOL_HIST:, cols]
            mixed = _dot(pooled.astype(BF16), wg_ref[gi]) * sc_ref[:, cols]
            mix_scr[:, cols] = mixed.astype(BF16)

    o_ref[...] = x_ref[...] + _dot(mix_scr[...], wo_ref[...])


def _pool_mix(u, x, w_grp, scale, w_out, seq, *, tm=512, tn=1024):
    t, d = u.shape
    hb = tm // _POOL_HIST
    return pl.pallas_call(
        functools.partial(_pool_kernel, tm=tm, seq=seq),
        grid=(t // tm, d // tn),
        in_specs=[
            pl.BlockSpec((tm, d), lambda i, j: (i, 0)),
            pl.BlockSpec((_POOL_HIST, d), lambda i, j: (jnp.maximum(i * hb - 1, 0), 0)),
            pl.BlockSpec((tm, tn), lambda i, j: (i, j)),
            pl.BlockSpec((N_POOL_GROUPS, POOL_GROUP, POOL_GROUP), lambda i, j: (0, 0, 0)),
            pl.BlockSpec((1, d), lambda i, j: (0, 0)),
            pl.BlockSpec((d, tn), lambda i, j: (0, j)),
        ],
        out_specs=pl.BlockSpec((tm, tn), lambda i, j: (i, j)),
        out_shape=jax.ShapeDtypeStruct((t, d), F32),
        scratch_shapes=[pltpu.VMEM((tm + _POOL_HIST, d), F32), pltpu.VMEM((tm, d), BF16)],
        compiler_params=_params(("parallel", "arbitrary"), 56),
        name="pool_mixer",
    )(u, u, x, w_grp, scale, w_out)


def _pad_heads(w, n, dh):
    d = w.shape[0]
    return jnp.pad(w.reshape(d, n, dh), ((0, 0), (0, 0), (0, HEAD_PAD - dh))).reshape(d, n * HEAD_PAD)


def _nsa_weights(w_in, rel_bias):
    g, hg = N_KV_GROUPS, HEADS_PER_GROUP
    qw, kw, vw = N_HEADS * D_K, g * D_K, g * D_V
    offs = np.cumsum([0, qw, kw, vw, kw, vw, kw, vw])
    wq, wkc, wvc, wks, wvs, wkw, wvw = [w_in[:, offs[i]:offs[i + 1]] for i in range(7)]
    wgt = w_in[:, offs[7]:]
    w_heads = jnp.concatenate([
        _pad_heads(wq, N_HEADS, D_K), _pad_heads(wks, g, D_K), _pad_heads(wkw, g, D_K),
        _pad_heads(wvs, g, D_V), _pad_heads(wvw, g, D_V)], axis=1).astype(BF16)
    scale = np.ones((N_SLOTS, HEAD_PAD), np.float32)
    scale[:N_HEADS] = D_K ** -0.5 * LOG2E
    add = np.zeros((N_SLOTS, HEAD_PAD), np.float32)
    add[SLOT_KS:SLOT_VS, D_K] = 1.0
    add[SLOT_VS:, D_V] = 1.0
    add = jnp.asarray(add).at[:N_HEADS, D_K].set(rel_bias[N_BUCKETS - 1].astype(F32) * LOG2E)
    wg = wgt.reshape(-1, 3, g, hg).transpose(0, 2, 1, 3).reshape(-1, g, 3 * hg)
    wg = jnp.pad(wg, ((0, 0), (0, 0), (0, LANES - 3 * hg))).reshape(-1, g * LANES)
    w_side = jnp.concatenate([wkc, wvc, wg], axis=1).astype(BF16)
    return w_heads, jnp.asarray(scale).reshape(1, -1), add.reshape(1, -1), w_side


def _overlap_matrix_t(nc, ns):
    i = np.arange(nc)[None, :]
    j = np.arange(ns)[:, None]
    ratio = SLC_BLOCK // CMP_STRIDE
    diff = i - ratio * j
    w = np.zeros((ns, nc), np.float32)
    for n in range(CMP_BLOCK // CMP_STRIDE):
        w += ((diff + n >= 0) & (diff + n < ratio)).astype(np.float32)
    w[:, nc - 1] = 0.0
    return jnp.asarray(w, dtype=BF16)


def _nsa_layer(x, gamma, rel_bias, w_in, pos_k, pos_v, ck_w1, ck_b1, ck_w2, cv_w1, cv_b1, cv_w2,
               w_out, batch):
    t = x.shape[0]
    seq = t // batch
    assert seq % Q_TILE == 0
    w_heads, sc_row, ad_row, w_side = _nsa_weights(w_in, rel_bias)
    tb, lc = _build_tables(rel_bias)
    qkv = _norm_heads(x, gamma, w_heads, sc_row, ad_row)
    kc, vc, gates = _norm_side(x, gamma, w_side)
    kcmp = _compress(kc, pos_k, ck_w1, ck_b1, ck_w2, D_K, D_K, batch)
    vcmp = _compress(vc, pos_v, cv_w1, cv_b1, cv_w2, D_V, None, batch)
    ovt = _overlap_matrix_t(seq // CMP_STRIDE, seq // SLC_BLOCK)
    o_c, sel = _cmp_attention(qkv, lc, kcmp, vcmp, ovt, gates, batch)
    o_s = _sel_attention(qkv, sel, tb, gates, batch)
    o_w = _win_attention(qkv, tb, gates, batch)
    return _out_proj(o_c, o_s, o_w, w_out.astype(BF16), x)


def kernel(x, norm_mix, norm_ffn, norm_final, rel_bias, nsa_w_in, nsa_pos_k, nsa_pos_v, nsa_ck_w1, nsa_ck_b1, nsa_ck_w2, nsa_cv_w1, nsa_cv_b1, nsa_cv_w2, nsa_w_out, pool_w_in, pool_w_grp, pool_scale, pool_w_out, ffn_w_gu, ffn_w_down):
    batch, seq, d = x.shape
    h = x.reshape(batch * seq, d)
    gfin = norm_final.reshape(1, d)
    h = _nsa_layer(h, norm_mix[0].reshape(1, d), rel_bias, nsa_w_in[0], nsa_pos_k[0], nsa_pos_v[0],
                   nsa_ck_w1[0], nsa_ck_b1[0], nsa_ck_w2[0], nsa_cv_w1[0], nsa_cv_b1[0], nsa_cv_w2[0],
                   nsa_w_out[0], batch)
    h = _ffn(h, norm_ffn[0].reshape(1, d), ffn_w_gu[0].astype(BF16), ffn_w_down[0].astype(BF16), gfin,
             final_norm=False)
    u = _norm_mm(h, norm_mix[1].reshape(1, d), pool_w_in[0].astype(BF16))
    h = _pool_mix(u, h, pool_w_grp[0].astype(BF16), pool_scale[0].reshape(1, d),
                  pool_w_out[0].astype(BF16), seq)
    h = _ffn(h, norm_ffn[1].reshape(1, d), ffn_w_gu[1].astype(BF16), ffn_w_down[1].astype(BF16), gfin,
             final_norm=True)
    return h.reshape(batch, seq, d)
```

```python
import functools
import math

import numpy as np
import jax
import jax.numpy as jnp
from jax import lax
from jax.experimental import pallas as pl
from jax.experimental.pallas import tpu as pltpu

D_MODEL = 2048
N_HEADS = 32
N_KV_GROUPS = 4
HEADS_PER_GROUP = N_HEADS // N_KV_GROUPS
D_K = 96
D_V = 64
CMP_BLOCK = 32
CMP_STRIDE = 16
CMP_HIDDEN = 256
SLC_BLOCK = 64
N_SLC = 16
WINDOW = 512
N_BUCKETS = 32
MAX_DISTANCE = 128
POOL_WINDOWS = (2, 4, 8, 16)
N_POOL_GROUPS = len(POOL_WINDOWS)
POOL_GROUP = D_MODEL // N_POOL_GROUPS
D_FF = -(-8 * D_MODEL // (3 * 256)) * 256
EPS = 1e-6
NEG_INF = -1e30
FORCE_SCORE = 1e9
LOG2E = math.log2(math.e)

LANES = 128
HEAD_PAD = LANES
Q_TILE = 256
KEY_CHUNK = 256
CMP_PER_TILE = Q_TILE // CMP_STRIDE
BIAS_K = CMP_PER_TILE * 8
SLOT_Q = 0
SLOT_KS = N_HEADS
SLOT_KW = SLOT_KS + N_KV_GROUPS
SLOT_VS = SLOT_KW + N_KV_GROUPS
SLOT_VW = SLOT_VS + N_KV_GROUPS
N_SLOTS = SLOT_VW + N_KV_GROUPS

F32 = jnp.float32
BF16 = jnp.bfloat16


def _bucket_np(dist):
    n = np.maximum(dist, 0)
    max_exact = N_BUCKETS // 2
    nf = np.maximum(n, 1).astype(np.float32)
    scaled = (np.log(nf / np.float32(max_exact)) / np.float32(math.log(MAX_DISTANCE / max_exact))
              * np.float32(N_BUCKETS - max_exact)).astype(np.float32)
    large = np.minimum(max_exact + scaled.astype(np.int32), N_BUCKETS - 1)
    return np.where(n < max_exact, n, large).astype(np.int32)


_FAR_DIST = int(np.max(np.nonzero(_bucket_np(np.arange(4096)) != N_BUCKETS - 1)[0])) + 1
assert _FAR_DIST <= CMP_STRIDE * 8 + (CMP_STRIDE - 1) - (CMP_BLOCK - 1) + 1 and _FAR_DIST <= KEY_CHUNK
_MASKED_BUCKET = N_BUCKETS
_ZERO_BUCKET = N_BUCKETS + 1


def _toeplitz_bucket_map():
    q = np.arange(Q_TILE)[:, None]
    r = np.arange(2 * KEY_CHUNK)[None, :]
    d = q + KEY_CHUNK - r
    b = _bucket_np(d)
    b = np.where(d >= _FAR_DIST, _ZERO_BUCKET, b)
    return np.where(d < 0, _MASKED_BUCKET, b).astype(np.int32)


def _cmp_bucket_map():
    q = np.arange(Q_TILE)[:, None]
    k = np.arange(BIAS_K)[None, :]
    a, b = q // CMP_STRIDE, q % CMP_STRIDE
    a2, m = k // 8, k % 8 + 1
    d = CMP_STRIDE * m + b - (CMP_BLOCK - 1)
    bk = _bucket_np(d)
    bk = np.where((d >= _FAR_DIST) | (d < 0) | (a != a2), _ZERO_BUCKET, bk)
    return bk.astype(np.int32)


def _params(sem, vmem_mb):
    return pltpu.CompilerParams(dimension_semantics=sem, vmem_limit_bytes=vmem_mb * 1024 * 1024)


def _rms(x, g):
    ms = jnp.mean(x * x, axis=-1, keepdims=True)
    return x * lax.rsqrt(ms + EPS) * g


def _dot(a, b):
    return jnp.dot(a, b, preferred_element_type=F32)


def _dot_nt(a, b):
    return lax.dot_general(a, b, (((1,), (1,)), ((), ())), preferred_element_type=F32)


def _split_rows(dot, a, b):
    half = a.shape[0] // 2
    return jnp.concatenate([dot(a[:half], b), dot(a[half:], b)], axis=0)


def _store_normalised(o_ref, head_acc, gates, gate_col):
    assert 2 * D_V == LANES
    left = lax.broadcasted_iota(jnp.int32, (Q_TILE, LANES), 1) < D_V
    for pair in range(HEADS_PER_GROUP // 2):
        h0, h1 = 2 * pair, 2 * pair + 1
        a0, a1 = head_acc(h0), head_acc(h1)
        num = jnp.where(left, a0, pltpu.roll(a1, D_V, 1))
        den = jnp.where(left, a0[:, D_V:D_V + 1], a1[:, D_V:D_V + 1])
        gate = jnp.where(left, gates[:, gate_col + h0:gate_col + h0 + 1], gates[:, gate_col + h1:gate_col + h1 + 1])
        o_ref[:, pair * LANES:(pair + 1) * LANES] = (num / den * gate).astype(o_ref.dtype)


def _tables_kernel(tbl_ref, bm_ref, bc_ref, tb_ref, lc_ref):
    h = pl.program_id(0) * HEADS_PER_GROUP + pl.program_id(1)
    far = tbl_ref[N_BUCKETS - 1, h]
    bm = bm_ref[...]
    bc = bc_ref[...]
    tb = jnp.where(bm == _MASKED_BUCKET, NEG_INF, 0.0).astype(F32)
    lc = jnp.zeros(bc.shape, F32)
    for b in range(N_BUCKETS - 1):
        val = (tbl_ref[b, h] - far) * LOG2E
        tb = jnp.where(bm == b, val, tb)
        lc = jnp.where(bc == b, val, lc)
    tb_ref[0, _TB_NEAR] = tb[:, :KEY_CHUNK]
    tb_ref[0, _TB_DIAG] = tb[:, KEY_CHUNK:]
    lc_ref[0] = lc.astype(BF16)


_TB_NEAR, _TB_DIAG = 0, 1
_TB_SECTIONS = 2


def _build_tables(rel_bias):
    bm = jnp.asarray(_toeplitz_bucket_map())
    bc = jnp.asarray(_cmp_bucket_map())
    g, hg = N_KV_GROUPS, HEADS_PER_GROUP
    return pl.pallas_call(
        _tables_kernel,
        grid=(g, hg),
        in_specs=[
            pl.BlockSpec(memory_space=pltpu.SMEM),
            pl.BlockSpec(bm.shape, lambda i, j: (0, 0)),
            pl.BlockSpec(bc.shape, lambda i, j: (0, 0)),
        ],
        out_specs=[
            pl.BlockSpec((1, _TB_SECTIONS, Q_TILE, KEY_CHUNK), lambda i, j: (i, 0, j, 0)),
            pl.BlockSpec((1, Q_TILE, BIAS_K), lambda i, j: (i, j, 0)),
        ],
        out_shape=[
            jax.ShapeDtypeStruct((g, _TB_SECTIONS, hg * Q_TILE, KEY_CHUNK), F32),
            jax.ShapeDtypeStruct((g, hg * Q_TILE, BIAS_K), BF16),
        ],
        compiler_params=_params(("arbitrary", "arbitrary"), 32),
        name="bias_tables",
    )(rel_bias, bm, bc)


def _norm_heads_kernel(x_ref, g_ref, w_ref, sc_ref, ad_ref, o_ref, h_scr, *, slabs):
    @pl.when(pl.program_id(1) == 0)
    def _():
        h_scr[...] = _rms(x_ref[...], g_ref[...]).astype(BF16)

    acc = _dot(h_scr[...], w_ref[...]) * sc_ref[...] + ad_ref[...]
    for k in range(slabs):
        o_ref[k] = acc[:, k * HEAD_PAD:(k + 1) * HEAD_PAD].astype(o_ref.dtype)


def _norm_heads(x, gamma, w, scale_row, add_row, *, tm=1024, slabs=8):
    t, d = x.shape
    n = w.shape[1]
    tn = slabs * HEAD_PAD
    return pl.pallas_call(
        functools.partial(_norm_heads_kernel, slabs=slabs),
        grid=(t // tm, n // tn),
        in_specs=[
            pl.BlockSpec((tm, d), lambda i, j: (i, 0)),
            pl.BlockSpec((1, d), lambda i, j: (0, 0)),
            pl.BlockSpec((d, tn), lambda i, j: (0, j)),
            pl.BlockSpec((1, tn), lambda i, j: (0, j)),
            pl.BlockSpec((1, tn), lambda i, j: (0, j)),
        ],
        out_specs=pl.BlockSpec((slabs, tm, HEAD_PAD), lambda i, j: (j, i, 0)),
        out_shape=jax.ShapeDtypeStruct((n // HEAD_PAD, t, HEAD_PAD), BF16),
        scratch_shapes=[pltpu.VMEM((tm, d), BF16)],
        compiler_params=_params(("parallel", "arbitrary"), 48),
        name="norm_proj_heads",
    )(x, gamma, w, scale_row, add_row)


def _norm_side_kernel(x_ref, g_ref, w_ref, kc_ref, vc_ref, gt_ref, *, kw, vw):
    h = _rms(x_ref[...], g_ref[...]).astype(BF16)
    acc = _dot(h, w_ref[...])
    kc_ref[...] = acc[:, :kw]
    vc_ref[...] = acc[:, kw:kw + vw]
    for g in range(N_KV_GROUPS):
        lo = kw + vw + g * LANES
        gt_ref[g] = jax.nn.sigmoid(acc[:, lo:lo + LANES])


def _norm_side(x, gamma, w, *, tm=512):
    t, d = x.shape
    kw, vw = N_KV_GROUPS * D_K, N_KV_GROUPS * D_V
    n = w.shape[1]
    return pl.pallas_call(
        functools.partial(_norm_side_kernel, kw=kw, vw=vw),
        grid=(t // tm,),
        in_specs=[
            pl.BlockSpec((tm, d), lambda i: (i, 0)),
            pl.BlockSpec((1, d), lambda i: (0, 0)),
            pl.BlockSpec((d, n), lambda i: (0, 0)),
        ],
        out_specs=[
            pl.BlockSpec((tm, kw), lambda i: (i, 0)),
            pl.BlockSpec((tm, vw), lambda i: (i, 0)),
            pl.BlockSpec((N_KV_GROUPS, tm, LANES), lambda i: (0, i, 0)),
        ],
        out_shape=[
            jax.ShapeDtypeStruct((t, kw), F32),
            jax.ShapeDtypeStruct((t, vw), F32),
            jax.ShapeDtypeStruct((N_KV_GROUPS, t, LANES), F32),
        ],
        compiler_params=_params(("parallel",), 48),
        name="norm_proj_side",
    )(x, gamma, w)


def _norm_mm_kernel(x_ref, g_ref, w_ref, o_ref, h_scr):
    @pl.when(pl.program_id(1) == 0)
    def _():
        h_scr[...] = _rms(x_ref[...], g_ref[...]).astype(BF16)

    o_ref[...] = _dot(h_scr[...], w_ref[...])


def _norm_mm(x, gamma, w, *, tm=512, tn=1024):
    t, d = x.shape
    n = w.shape[1]
    return pl.pallas_call(
        _norm_mm_kernel,
        grid=(t // tm, n // tn),
        in_specs=[
            pl.BlockSpec((tm, d), lambda i, j: (i, 0)),
            pl.BlockSpec((1, d), lambda i, j: (0, 0)),
            pl.BlockSpec((d, tn), lambda i, j: (0, j)),
        ],
        out_specs=pl.BlockSpec((tm, tn), lambda i, j: (i, j)),
        out_shape=jax.ShapeDtypeStruct((t, n), F32),
        scratch_shapes=[pltpu.VMEM((tm, d), BF16)],
        compiler_params=_params(("parallel", "arbitrary"), 48),
        name="norm_matmul",
    )(x, gamma, w)


def _compress_kernel(x_ref, pa_ref, pb_ref, wa_ref, wb_ref, b1_ref, w2_ref, ad_ref, o_ref,
                     acca, accb, *, nk):
    k = pl.program_id(1)

    @pl.when(k == 0)
    def _():
        acca[...] = jnp.zeros_like(acca)
        accb[...] = jnp.zeros_like(accb)

    x = x_ref[...]
    acca[...] += _dot((x + pa_ref[...]).astype(BF16), wa_ref[...])
    accb[...] += _dot((x + pb_ref[...]).astype(BF16), wb_ref[...])

    @pl.when(k == nk - 1)
    def _():
        rows = accb.shape[0]
        hid = acca[...] + pltpu.roll(accb[...], rows - 1, 0) + b1_ref[...]
        hid = jax.nn.gelu(hid)
        out = _dot(hid.astype(BF16), w2_ref[...]) + ad_ref[...]
        for g in range(N_KV_GROUPS):
            o_ref[g] = out[:, g * HEAD_PAD:(g + 1) * HEAD_PAD].astype(o_ref.dtype)


def _compress(kv, pos, w1, b1, w2, dh, one_col, batch):
    t = kv.shape[0]
    g = N_KV_GROUPS
    chunks = t // CMP_STRIDE
    rows = chunks // batch
    width = CMP_STRIDE * g * dh
    x = kv.reshape(chunks, width)
    eye = jnp.eye(g, dtype=F32)
    w1r = w1.reshape(2, CMP_STRIDE, dh, CMP_HIDDEN)

    def expand(wh):
        return jnp.einsum('rdj,gh->rgdhj', wh, eye).reshape(width, g * CMP_HIDDEN).astype(BF16)

    wa, wb = expand(w1r[0]), expand(w1r[1])
    posr = pos.reshape(2, CMP_STRIDE, 1, dh)
    pa = jnp.broadcast_to(posr[0], (CMP_STRIDE, g, dh)).reshape(1, width)
    pb = jnp.broadcast_to(posr[1], (CMP_STRIDE, g, dh)).reshape(1, width)
    b1r = jnp.tile(b1.reshape(1, CMP_HIDDEN), (1, g))
    w2p = jnp.pad(w2, ((0, 0), (0, HEAD_PAD - dh)))
    w2x = jnp.einsum('jd,gh->gjhd', w2p, eye).reshape(g * CMP_HIDDEN, g * HEAD_PAD).astype(BF16)
    ad = np.zeros((1, g * HEAD_PAD), np.float32)
    if one_col is not None:
        ad[0, one_col::HEAD_PAD] = 1.0
    ad = jnp.asarray(ad)
    kt = width // 8
    nk = width // kt
    return pl.pallas_call(
        functools.partial(_compress_kernel, nk=nk),
        grid=(batch, nk),
        in_specs=[
            pl.BlockSpec((rows, kt), lambda b, k: (b, k)),
            pl.BlockSpec((1, kt), lambda b, k: (0, k)),
            pl.BlockSpec((1, kt), lambda b, k: (0, k)),
            pl.BlockSpec((kt, g * CMP_HIDDEN), lambda b, k: (k, 0)),
            pl.BlockSpec((kt, g * CMP_HIDDEN), lambda b, k: (k, 0)),
            pl.BlockSpec((1, g * CMP_HIDDEN), lambda b, k: (0, 0)),
            pl.BlockSpec((g * CMP_HIDDEN, g * HEAD_PAD), lambda b, k: (0, 0)),
            pl.BlockSpec((1, g * HEAD_PAD), lambda b, k: (0, 0)),
        ],
        out_specs=pl.BlockSpec((g, rows, HEAD_PAD), lambda b, k: (0, b, 0)),
        out_shape=jax.ShapeDtypeStruct((g, chunks, HEAD_PAD), BF16),
        scratch_shapes=[pltpu.VMEM((rows, g * CMP_HIDDEN), F32),
                        pltpu.VMEM((rows, g * CMP_HIDDEN), F32)],
        compiler_params=_params(("parallel", "arbitrary"), 48),
        name="compress_mlp",
    )(x, pa, pb, wa, wb, b1r, w2x, ad)


_CMP_MASK_LANE = D_K + 1
_CMP_THR_MIN = -((CMP_BLOCK - 1 + CMP_STRIDE - 1) // CMP_STRIDE)
_CMP_MASK_N = (Q_TILE - CMP_BLOCK) // CMP_STRIDE - _CMP_THR_MIN + 1
assert _CMP_MASK_LANE + _CMP_MASK_N <= HEAD_PAD


def _cmp_query_mask():
    q = np.arange(Q_TILE)
    thr = (q - (CMP_BLOCK - 1)) // CMP_STRIDE
    m = np.zeros((Q_TILE, HEAD_PAD), np.float32)
    m[q, _CMP_MASK_LANE + thr - _CMP_THR_MIN] = 1.0
    return jnp.asarray(np.tile(m, (HEADS_PER_GROUP, 1)), dtype=BF16)


def _cmp_kernel(q_ref, qm_ref, lc_ref, kc_ref, vc_ref, ovt_ref, gt_ref, o_ref, sel_ref,
                s_scr, p_scr, psum_scr, selt_scr, *, n_sel):
    tq = Q_TILE
    t = pl.program_id(2)
    t0 = t * tq
    nc = kc_ref.shape[1]
    ns = ovt_ref.shape[0]
    ii = lax.broadcasted_iota(jnp.int32, (nc, LANES), 0) - CMP_PER_TILE * t
    ll = lax.broadcasted_iota(jnp.int32, (nc, LANES), 1)
    hot = ii == ((ll >> 3) - ((ll & 7) + 1))
    late = ((ll >= _CMP_MASK_LANE) & (ll < _CMP_MASK_LANE + _CMP_MASK_N)
            & (ii > ll - (_CMP_MASK_LANE - _CMP_THR_MIN)))
    keys = jnp.where(late, NEG_INF, kc_ref[0].astype(F32)).astype(BF16)
    rhs = jnp.concatenate([keys, jnp.where(hot, 1.0, 0.0).astype(BF16)], axis=1)
    q = q_ref[...].reshape(HEADS_PER_GROUP * tq, HEAD_PAD) + qm_ref[...]
    lhs = jnp.concatenate([q, lc_ref[0]], axis=1)
    gt = gt_ref[0]
    n_chunk = max(nc // KEY_CHUNK, 1)
    wchunk = nc // n_chunk
    live = jnp.minimum((t0 + tq - CMP_BLOCK) // (CMP_STRIDE * wchunk) + 1, n_chunk)

    def attend(w):
        s_scr[:, :w] = _split_rows(_dot_nt, lhs, rhs[:w])
        for hh in range(HEADS_PER_GROUP):
            rows = slice(hh * tq, (hh + 1) * tq)
            s = s_scr[rows, :w]
            m = jnp.max(s, axis=-1, keepdims=True)
            e = jnp.exp2(s - m)
            l = jnp.sum(e, axis=-1, keepdims=True)
            p = e * jnp.where(m > 0.5 * NEG_INF, 1.0 / l, 0.0)
            if hh == 0:
                psum_scr[:, :w] = p
            else:
                psum_scr[:, :w] += p
            p_scr[rows, :w] = p.astype(BF16)
        o = _split_rows(_dot, p_scr[:, :w], vc_ref[0, :w, :])
        for hh in range(HEADS_PER_GROUP):
            o_ref[:, hh * D_V:(hh + 1) * D_V] = (o[hh * tq:(hh + 1) * tq, :D_V] * gt[:, hh:hh + 1]).astype(o_ref.dtype)
        nr = min(ns, w * CMP_STRIDE // SLC_BLOCK)
        psum = psum_scr[:, :w]
        p_hi = psum.astype(BF16)
        p_lo = (psum - p_hi.astype(F32)).astype(BF16)
        ovt = ovt_ref[:nr, :w]
        imp = _dot_nt(ovt, p_hi) + _dot_nt(ovt, p_lo)
        jr = lax.broadcasted_iota(jnp.int32, (nr, tq), 0)
        cur = (t0 + lax.broadcasted_iota(jnp.int32, (nr, tq), 1)) // SLC_BLOCK
        forced = (jr == 0) | (jr == cur) | (jr == cur - 1)
        n_forced = 3
        distinct = (t0 // SLC_BLOCK >= 2) & (n_sel > n_forced)
        imp = jnp.where(forced, jnp.where(distinct, -jnp.inf, FORCE_SCORE), imp)
        imp = jnp.where(jr > cur, -1.0, imp)
        jrf = jr.astype(F32)

        def pick(_, imp):
            m = jnp.max(imp, axis=0, keepdims=True)
            first = jnp.min(jnp.where(imp == m, jrf, 1e9), axis=0, keepdims=True)
            return jnp.where(jrf == first, -jnp.inf, imp)

        imp = lax.fori_loop(0, n_sel - jnp.where(distinct, n_forced, 0), pick, imp)
        selt_scr[0:nr] = jnp.where(imp == -jnp.inf, 1.0, 0.0)
        if nr < ns:
            selt_scr[nr:] = jnp.zeros((ns - nr, tq), F32)

    for k in range(1, n_chunk + 1):
        pl.when(live == k)(functools.partial(attend, k * wchunk))

    sel = selt_scr[...].T.astype(BF16)
    if sel_ref.shape[2] == ns:
        sel_ref[0] = sel
    else:
        sel_ref[0] = jnp.zeros(sel_ref.shape[1:], BF16)
        sel_ref[0, :, 0:ns] = sel


def _cmp_attention(qkv, lc, kcmp, vcmp, ovt, gates, batch):
    t = qkv.shape[1]
    s = t // batch
    nc = kcmp.shape[1] // batch
    ns = ovt.shape[0]
    nt = s // Q_TILE
    g, hg = N_KV_GROUPS, HEADS_PER_GROUP
    ns_pad = -(-ns // LANES) * LANES
    return pl.pallas_call(
        functools.partial(_cmp_kernel, n_sel=min(N_SLC, ns)),
        grid=(batch, g, nt),
        in_specs=[
            pl.BlockSpec((hg, Q_TILE, HEAD_PAD), lambda b, i, j: (i, b * nt + j, 0)),
            pl.BlockSpec((hg * Q_TILE, HEAD_PAD), lambda b, i, j: (0, 0)),
            pl.BlockSpec((1, hg * Q_TILE, BIAS_K), lambda b, i, j: (i, 0, 0)),
            pl.BlockSpec((1, nc, HEAD_PAD), lambda b, i, j: (i, b, 0)),
            pl.BlockSpec((1, nc, HEAD_PAD), lambda b, i, j: (i, b, 0)),
            pl.BlockSpec((ns, nc), lambda b, i, j: (0, 0)),
            pl.BlockSpec((1, Q_TILE, LANES), lambda b, i, j: (i, b * nt + j, 0)),
        ],
        out_specs=[
            pl.BlockSpec((Q_TILE, hg * D_V), lambda b, i, j: (b * nt + j, i)),
            pl.BlockSpec((1, Q_TILE, ns_pad), lambda b, i, j: (i, b * nt + j, 0)),
        ],
        out_shape=[
            jax.ShapeDtypeStruct((t, N_HEADS * D_V), BF16),
            jax.ShapeDtypeStruct((g, t, ns_pad), BF16),
        ],
        scratch_shapes=[pltpu.VMEM((hg * Q_TILE, nc), F32), pltpu.VMEM((hg * Q_TILE, nc), BF16),
                        pltpu.VMEM((Q_TILE, nc), F32), pltpu.VMEM((ns, Q_TILE), F32)],
        compiler_params=_params(("parallel", "parallel", "arbitrary"), 48),
        name="cmp_attention",
    )(qkv, _cmp_query_mask(), lc, kcmp, vcmp, ovt, gates)


_FAR_FACTOR = 4


def _sel_kernel(q_ref, ks_ref, vs_ref, sel_ref, tb_ref, gt_ref, o_ref,
                lhs_scr, s_scr, smax_scr, m_scr, acc_scr):
    tq = Q_TILE
    t = pl.program_id(2)
    n_var = sel_ref.shape[2] // LANES
    chunks_per_var = LANES * SLC_BLOCK // KEY_CHUNK
    blocks_per_chunk = KEY_CHUNK // SLC_BLOCK
    unsel = (1.0 - sel_ref[0].astype(F32)).astype(BF16)
    def build_lhs(var):
        for hh in range(HEADS_PER_GROUP):
            rows = slice(hh * tq, (hh + 1) * tq)
            lhs_scr[var, rows, 0:HEAD_PAD] = q_ref[hh]
            lhs_scr[var, rows, HEAD_PAD:] = unsel[:, var * LANES:(var + 1) * LANES]

    build_lhs(0)
    m_scr[...] = jnp.full(m_scr.shape, NEG_INF, F32)
    acc_scr[...] = jnp.zeros(acc_scr.shape, F32)
    lane = lax.broadcasted_iota(jnp.int32, (KEY_CHUNK, LANES), 1)
    kblk = lax.broadcasted_iota(jnp.int32, (KEY_CHUNK, LANES), 0) // SLC_BLOCK

    n_far = jnp.maximum(t - 1, 0)
    n_wide = n_far // _FAR_FACTOR
    wide = _FAR_FACTOR * KEY_CHUNK
    lane_w = lax.broadcasted_iota(jnp.int32, (wide, LANES), 1)
    kblk_w = lax.broadcasted_iota(jnp.int32, (wide, LANES), 0) // SLC_BLOCK

    def qk(start, nkeys, first_block, phantom):
        k = ks_ref[0, pl.ds(start, nkeys), :]
        flag = first_block % LANES + (kblk if nkeys == KEY_CHUNK else kblk_w)
        armed = (lane if nkeys == KEY_CHUNK else lane_w) == flag
        if phantom is not None:
            armed = armed | phantom
        rhs = jnp.concatenate([k, jnp.where(armed, NEG_INF, 0.0).astype(BF16)], axis=1)
        return _dot_nt(lhs_scr[first_block // LANES], rhs)

    def accumulate(start, nkeys, s, smax):
        v = vs_ref[0, pl.ds(start, nkeys), :]
        m_old = m_scr[...]
        m_new = jnp.maximum(m_old, smax)
        alpha = jnp.exp2(m_old - m_new)
        p = jnp.exp2(s - jnp.concatenate([m_new] * (nkeys // LANES), axis=1))
        acc_scr[...] = alpha * acc_scr[...] + _dot(p.astype(BF16), v)
        m_scr[...] = m_new

    def wide_start(i):
        return pl.multiple_of(jnp.minimum(i, t // _FAR_FACTOR) * wide, wide)

    def scores(i, buf):
        ii = jnp.minimum(i, t // _FAR_FACTOR)
        s = qk(wide_start(i), wide, ii * (wide // SLC_BLOCK), i >= n_wide)
        s_scr[buf] = s
        smax_scr[buf] = jnp.broadcast_to(jnp.max(s, axis=-1, keepdims=True), smax_scr.shape[1:])

    def update(i, buf):
        accumulate(wide_start(i), wide, s_scr[buf], smax_scr[buf])

    def narrow(c, section):
        start = pl.multiple_of(c * KEY_CHUNK, KEY_CHUNK)
        s = qk(start, KEY_CHUNK, c * blocks_per_chunk, None)
        if section is not None:
            s = s + tb_ref[0, section]
        accumulate(start, KEY_CHUNK, s, jnp.max(s, axis=-1, keepdims=True))

    scores(0, 0)
    for var in range(1, n_var):
        pl.when(t >= var * chunks_per_var)(functools.partial(build_lhs, var))

    def trip(_, base):
        for u in range(2):
            scores(base + u + 1, (u + 1) % 2)
            update(base + u, u % 2)
        return base + 2

    lax.fori_loop(0, n_wide // 2, trip, 0)

    @pl.when(n_wide % 2 == 1)
    def _():
        update(n_wide - 1, 0)

    for left in range(_FAR_FACTOR):
        @pl.when((t >= 1) & (n_far % _FAR_FACTOR == left))
        def _():
            for u in range(left):
                narrow(n_wide * _FAR_FACTOR + u, None)
            narrow(t - 1, _TB_NEAR)
            narrow(t, _TB_DIAG)

    @pl.when(t == 0)
    def _():
        narrow(t, _TB_DIAG)
    _store_normalised(o_ref, lambda hh: acc_scr[hh * tq:(hh + 1) * tq], gt_ref[0], HEADS_PER_GROUP)


def _sel_attention(qkv, sel, tb, gates, batch):
    t = qkv.shape[1]
    s = t // batch
    ns = sel.shape[2]
    nt = s // Q_TILE
    g, hg = N_KV_GROUPS, HEADS_PER_GROUP
    assert ns % LANES == 0 and HEAD_PAD == LANES
    assert s % (_FAR_FACTOR * KEY_CHUNK) == 0
    return pl.pallas_call(
        _sel_kernel,
        grid=(batch, g, nt),
        in_specs=[
            pl.BlockSpec((hg, Q_TILE, HEAD_PAD), lambda b, i, j: (i, b * nt + j, 0)),
            pl.BlockSpec((1, s, HEAD_PAD), lambda b, i, j: (SLOT_KS + i, b, 0)),
            pl.BlockSpec((1, s, HEAD_PAD), lambda b, i, j: (SLOT_VS + i, b, 0)),
            pl.BlockSpec((1, Q_TILE, ns), lambda b, i, j: (i, b * nt + j, 0)),
            pl.BlockSpec((1, _TB_SECTIONS, hg * Q_TILE, KEY_CHUNK), lambda b, i, j: (i, 0, 0, 0)),
            pl.BlockSpec((1, Q_TILE, LANES), lambda b, i, j: (i, b * nt + j, 0)),
        ],
        out_specs=pl.BlockSpec((Q_TILE, hg * D_V), lambda b, i, j: (b * nt + j, i)),
        out_shape=jax.ShapeDtypeStruct((t, N_HEADS * D_V), BF16),
        scratch_shapes=[pltpu.VMEM((ns // LANES, hg * Q_TILE, HEAD_PAD + LANES), BF16),
                        pltpu.VMEM((2, hg * Q_TILE, _FAR_FACTOR * KEY_CHUNK), F32),
                        pltpu.VMEM((2, hg * Q_TILE, LANES), F32),
                        pltpu.VMEM((hg * Q_TILE, LANES), F32),
                        pltpu.VMEM((hg * Q_TILE, HEAD_PAD), F32)],
        compiler_params=_params(("parallel", "parallel", "arbitrary"), 60),
        name="sel_attention",
    )(qkv, qkv, qkv, sel, tb, gates)


def _win_kernel(q_ref, k0_ref, k1_ref, k2_ref, v0_ref, v1_ref, v2_ref, tb_ref, mw_ref, gt_ref, o_ref):
    tq = Q_TILE
    t = pl.program_id(2)
    pen0 = jnp.where(t >= 2, 0.0, NEG_INF).astype(F32)
    pen1 = jnp.where(t >= 1, 0.0, NEG_INF).astype(F32)
    q = q_ref[...].reshape(HEADS_PER_GROUP * tq, HEAD_PAD)
    s0 = _split_rows(_dot_nt, q, k0_ref[0]) + (mw_ref[...] + pen0)
    s1 = _split_rows(_dot_nt, q, k1_ref[0]) + (tb_ref[0, _TB_NEAR] + pen1)
    s2 = _split_rows(_dot_nt, q, k2_ref[0]) + tb_ref[0, _TB_DIAG]
    m = jnp.maximum(jnp.maximum(jnp.max(s0, axis=-1, keepdims=True),
                                jnp.max(s1, axis=-1, keepdims=True)),
                    jnp.max(s2, axis=-1, keepdims=True))
    acc = (_split_rows(_dot, jnp.exp2(s0 - m).astype(BF16), v0_ref[0])
           + _split_rows(_dot, jnp.exp2(s1 - m).astype(BF16), v1_ref[0])
           + _split_rows(_dot, jnp.exp2(s2 - m).astype(BF16), v2_ref[0]))
    _store_normalised(o_ref, lambda hh: acc[hh * tq:(hh + 1) * tq], gt_ref[0], 2 * HEADS_PER_GROUP)


def _win_attention(qkv, tb, gates, batch):
    t = qkv.shape[1]
    s = t // batch
    nt = s // Q_TILE
    g, hg = N_KV_GROUPS, HEADS_PER_GROUP
    assert WINDOW == 2 * KEY_CHUNK and Q_TILE == KEY_CHUNK
    q = np.arange(Q_TILE)[:, None]
    r = np.arange(KEY_CHUNK)[None, :]
    mw = jnp.asarray(np.tile(np.where(r > q, 0.0, NEG_INF).astype(np.float32), (hg, 1)))

    def kv_spec(slot, back):
        return pl.BlockSpec((1, KEY_CHUNK, HEAD_PAD),
                            lambda b, i, j: (slot + i, b * nt + jnp.maximum(j - back, 0), 0))

    return pl.pallas_call(
        _win_kernel,
        grid=(batch, g, nt),
        in_specs=[
            pl.BlockSpec((hg, Q_TILE, HEAD_PAD), lambda b, i, j: (i, b * nt + j, 0)),
            kv_spec(SLOT_KW, 2), kv_spec(SLOT_KW, 1), kv_spec(SLOT_KW, 0),
            kv_spec(SLOT_VW, 2), kv_spec(SLOT_VW, 1), kv_spec(SLOT_VW, 0),
            pl.BlockSpec((1, 2, hg * Q_TILE, KEY_CHUNK), lambda b, i, j: (i, 0, 0, 0)),
            pl.BlockSpec((hg * Q_TILE, KEY_CHUNK), lambda b, i, j: (0, 0)),
            pl.BlockSpec((1, Q_TILE, LANES), lambda b, i, j: (i, b * nt + j, 0)),
        ],
        out_specs=pl.BlockSpec((Q_TILE, hg * D_V), lambda b, i, j: (b * nt + j, i)),
        out_shape=jax.ShapeDtypeStruct((t, N_HEADS * D_V), BF16),
        compiler_params=_params(("parallel", "parallel", "arbitrary"), 48),
        name="win_attention",
    )(qkv, qkv, qkv, qkv, qkv, qkv, qkv, tb, mw, gates)


def _out_proj_kernel(a_ref, b_ref, c_ref, w_ref, x_ref, o_ref, s_scr):
    @pl.when(pl.program_id(1) == 0)
    def _():
        s_scr[...] = (a_ref[...].astype(F32) + b_ref[...].astype(F32) + c_ref[...].astype(F32)).astype(BF16)

    o_ref[...] = x_ref[...] + _dot(s_scr[...], w_ref[...])


def _out_proj(a, b, c, w, x, *, tm=512, tn=1024):
    t, d = a.shape
    n = w.shape[1]
    row = pl.BlockSpec((tm, d), lambda i, j: (i, 0))
    return pl.pallas_call(
        _out_proj_kernel,
        grid=(t // tm, n // tn),
        in_specs=[row, row, row,
                  pl.BlockSpec((d, tn), lambda i, j: (0, j)),
                  pl.BlockSpec((tm, tn), lambda i, j: (i, j))],
        out_specs=pl.BlockSpec((tm, tn), lambda i, j: (i, j)),
        out_shape=jax.ShapeDtypeStruct((t, n), F32),
        scratch_shapes=[pltpu.VMEM((tm, d), BF16)],
        compiler_params=_params(("parallel", "arbitrary"), 48),
        name="attn_out_proj",
    )(a, b, c, w, x)


def _ffn_kernel(x_ref, g_ref, wg_ref, wu_ref, wd_ref, gf_ref, o_ref, h_scr, acc_scr, *, nf, final_norm):
    f = pl.program_id(1)

    @pl.when(f == 0)
    def _():
        h_scr[...] = _rms(x_ref[...], g_ref[...]).astype(BF16)
        acc_scr[...] = jnp.zeros_like(acc_scr)

    h = h_scr[...]
    gate = _dot(h, wg_ref[...])
    up = _dot(h, wu_ref[...])
    act = (jax.nn.silu(gate) * up).astype(BF16)
    acc_scr[...] += _dot(act, wd_ref[...])

    @pl.when(f == nf - 1)
    def _():
        y = x_ref[...] + acc_scr[...]
        if final_norm:
            y = _rms(y, gf_ref[...])
        o_ref[...] = y


def _ffn(x, gamma, w_gu, w_down, gamma_final, *, final_norm, tm=512, tf=512):
    t, d = x.shape
    nf = D_FF // tf
    return pl.pallas_call(
        functools.partial(_ffn_kernel, nf=nf, final_norm=final_norm),
        grid=(t // tm, nf),
        in_specs=[
            pl.BlockSpec((tm, d), lambda i, f: (i, 0)),
            pl.BlockSpec((1, d), lambda i, f: (0, 0)),
            pl.BlockSpec((d, tf), lambda i, f: (0, f)),
            pl.BlockSpec((d, tf), lambda i, f: (0, nf + f)),
            pl.BlockSpec((tf, d), lambda i, f: (f, 0)),
            pl.BlockSpec((1, d), lambda i, f: (0, 0)),
        ],
        out_specs=pl.BlockSpec((tm, d), lambda i, f: (i, 0)),
        out_shape=jax.ShapeDtypeStruct((t, d), F32),
        scratch_shapes=[pltpu.VMEM((tm, d), BF16), pltpu.VMEM((tm, d), F32)],
        compiler_params=_params(("parallel", "arbitrary"), 56),
        name="swiglu_ffn",
    )(x, gamma, w_gu, w_gu, w_down, gamma_final)


_POOL_HIST = 16


def _pool_kernel(u_ref, up_ref, x_ref, wg_ref, sc_ref, wo_ref, o_ref, ext_scr, mix_scr, *, tm, seq):
    i = pl.program_id(0)

    @pl.when(pl.program_id(1) == 0)
    def _():
        t0 = (i * tm) % seq
        hist = up_ref[...]
        ext_scr[0:_POOL_HIST, :] = jnp.where(t0 == 0, jnp.zeros_like(hist), hist)
        ext_scr[_POOL_HIST:, :] = u_ref[...]
        tpos = t0 + lax.broadcasted_iota(jnp.int32, (tm, 1), 0)
        for gi, w in enumerate(POOL_WINDOWS):
            cols = slice(gi * POOL_GROUP, (gi + 1) * POOL_GROUP)
            tot = ext_scr[_POOL_HIST:, cols]
            for back in range(1, w):
                tot = tot + ext_scr[_POOL_HIST - back:_POOL_HIST - back + tm, cols]
            cnt = jnp.minimum(tpos + 1, w).astype(F32)
            pooled = tot / cnt - ext_scr[_POOL_HIST:, cols]
            mixed = _dot(pooled.astype(BF16), wg_ref[gi]) * sc_ref[:, cols]
            mix_scr[:, cols] = mixed.astype(BF16)

    o_ref[...] = x_ref[...] + _dot(mix_scr[...], wo_ref[...])


def _pool_mix(u, x, w_grp, scale, w_out, seq, *, tm=512, tn=1024):
    t, d = u.shape
    hb = tm // _POOL_HIST
    return pl.pallas_call(
        functools.partial(_pool_kernel, tm=tm, seq=seq),
        grid=(t // tm, d // tn),
        in_specs=[
            pl.BlockSpec((tm, d), lambda i, j: (i, 0)),
            pl.BlockSpec((_POOL_HIST, d), lambda i, j: (jnp.maximum(i * hb - 1, 0), 0)),
            pl.BlockSpec((tm, tn), lambda i, j: (i, j)),
            pl.BlockSpec((N_POOL_GROUPS, POOL_GROUP, POOL_GROUP), lambda i, j: (0, 0, 0)),
            pl.BlockSpec((1, d), lambda i, j: (0, 0)),
            pl.BlockSpec((d, tn), lambda i, j: (0, j)),
        ],
        out_specs=pl.BlockSpec((tm, tn), lambda i, j: (i, j)),
        out_shape=jax.ShapeDtypeStruct((t, d), F32),
        scratch_shapes=[pltpu.VMEM((tm + _POOL_HIST, d), F32), pltpu.VMEM((tm, d), BF16)],
        compiler_params=_params(("parallel", "arbitrary"), 56),
        name="pool_mixer",
    )(u, u, x, w_grp, scale, w_out)


def _pad_heads(w, n, dh):
    d = w.shape[0]
    return jnp.pad(w.reshape(d, n, dh), ((0, 0), (0, 0), (0, HEAD_PAD - dh))).reshape(d, n * HEAD_PAD)


def _nsa_weights(w_in, rel_bias):
    g, hg = N_KV_GROUPS, HEADS_PER_GROUP
    qw, kw, vw = N_HEADS * D_K, g * D_K, g * D_V
    offs = np.cumsum([0, qw, kw, vw, kw, vw, kw, vw])
    wq, wkc, wvc, wks, wvs, wkw, wvw = [w_in[:, offs[i]:offs[i + 1]] for i in range(7)]
    wgt = w_in[:, offs[7]:]
    w_heads = jnp.concatenate([
        _pad_heads(wq, N_HEADS, D_K), _pad_heads(wks, g, D_K), _pad_heads(wkw, g, D_K),
        _pad_heads(wvs, g, D_V), _pad_heads(wvw, g, D_V)], axis=1).astype(BF16)
    scale = np.ones((N_SLOTS, HEAD_PAD), np.float32)
    scale[:N_HEADS] = D_K ** -0.5 * LOG2E
    add = np.zeros((N_SLOTS, HEAD_PAD), np.float32)
    add[SLOT_KS:SLOT_VS, D_K] = 1.0
    add[SLOT_VS:, D_V] = 1.0
    add = jnp.asarray(add).at[:N_HEADS, D_K].set(rel_bias[N_BUCKETS - 1].astype(F32) * LOG2E)
    wg = wgt.reshape(-1, 3, g, hg).transpose(0, 2, 1, 3).reshape(-1, g, 3 * hg)
    wg = jnp.pad(wg, ((0, 0), (0, 0), (0, LANES - 3 * hg))).reshape(-1, g * LANES)
    w_side = jnp.concatenate([wkc, wvc, wg], axis=1).astype(BF16)
    return w_heads, jnp.asarray(scale).reshape(1, -1), add.reshape(1, -1), w_side


def _overlap_matrix_t(nc, ns):
    i = np.arange(nc)[None, :]
    j = np.arange(ns)[:, None]
    ratio = SLC_BLOCK // CMP_STRIDE
    diff = i - ratio * j
    w = np.zeros((ns, nc), np.float32)
    for n in range(CMP_BLOCK // CMP_STRIDE):
        w += ((diff + n >= 0) & (diff + n < ratio)).astype(np.float32)
    w[:, nc - 1] = 0.0
    return jnp.asarray(w, dtype=BF16)


def _nsa_layer(x, gamma, rel_bias, w_in, pos_k, pos_v, ck_w1, ck_b1, ck_w2, cv_w1, cv_b1, cv_w2,
               w_out, batch):
    t = x.shape[0]
    seq = t // batch
    assert seq % Q_TILE == 0
    w_heads, sc_row, ad_row, w_side = _nsa_weights(w_in, rel_bias)
    tb, lc = _build_tables(rel_bias)
    qkv = _norm_heads(x, gamma, w_heads, sc_row, ad_row)
    kc, vc, gates = _norm_side(x, gamma, w_side)
    kcmp = _compress(kc, pos_k, ck_w1, ck_b1, ck_w2, D_K, D_K, batch)
    vcmp = _compress(vc, pos_v, cv_w1, cv_b1, cv_w2, D_V, None, batch)
    ovt = _overlap_matrix_t(seq // CMP_STRIDE, seq // SLC_BLOCK)
    o_c, sel = _cmp_attention(qkv, lc, kcmp, vcmp, ovt, gates, batch)
    o_s = _sel_attention(qkv, sel, tb, gates, batch)
    o_w = _win_attention(qkv, tb, gates, batch)
    return _out_proj(o_c, o_s, o_w, w_out.astype(BF16), x)


def kernel(x, norm_mix, norm_ffn, norm_final, rel_bias, nsa_w_in, nsa_pos_k, nsa_pos_v, nsa_ck_w1, nsa_ck_b1, nsa_ck_w2, nsa_cv_w1, nsa_cv_b1, nsa_cv_w2, nsa_w_out, pool_w_in, pool_w_grp, pool_scale, pool_w_out, ffn_w_gu, ffn_w_down):
    batch, seq, d = x.shape
    h = x.reshape(batch * seq, d)
    gfin = norm_final.reshape(1, d)
    h = _nsa_layer(h, norm_mix[0].reshape(1, d), rel_bias, nsa_w_in[0], nsa_pos_k[0], nsa_pos_v[0],
                   nsa_ck_w1[0], nsa_ck_b1[0], nsa_ck_w2[0], nsa_cv_w1[0], nsa_cv_b1[0], nsa_cv_w2[0],
                   nsa_w_out[0], batch)
    h = _ffn(h, norm_ffn[0].reshape(1, d), ffn_w_gu[0].astype(BF16), ffn_w_down[0].astype(BF16), gfin,
             final_norm=False)
    u = _norm_mm(h, norm_mix[1].reshape(1, d), pool_w_in[0].astype(BF16))
    h = _pool_mix(u, h, pool_w_grp[0].astype(BF16), pool_scale[0].reshape(1, d),
                  pool_w_out[0].astype(BF16), seq)
    h = _ffn(h, norm_ffn[1].reshape(1, d), ffn_w_gu[1].astype(BF16), ffn_w_down[1].astype(BF16), gfin,
             final_norm=True)
    return h.reshape(batch, seq, d)
```
